```python
import math
import jax, jax.numpy as jnp
from jax import lax
import numpy as np

D_MODEL = 1024
BATCH = 8
SEQ = 8192
DEPTH = 4
DEC_BATCH = 4
DEC_SEQ = 8192
PAST_LEN = 128

GRID_W = 64
N_MIXERS = 2
N_LRU_LAYERS = (DEPTH + 1) // 2
N_ATTN_LAYERS = DEPTH // 2
D_RNN = D_MODEL
LRU_BLOCKS = 8
LRU_BW = D_RNN // LRU_BLOCKS
CONV_W = 4
CONV_LEFT = 1
LRU_C = 8.0
HEAD_DIM = 128
N_Q_HEADS = D_MODEL // HEAD_DIM
N_KV_HEADS = 2
Q_PER_KV = N_Q_HEADS // N_KV_HEADS
ROPE_THETA = 10000.0
ROPE_AXIS_DIM = HEAD_DIM // 2
ROPE_PAIRS = ROPE_AXIS_DIM // 2
Q_BLOCK = 128
N_EXPERTS = 16
N_GROUPS = 4
GROUP_SIZE = N_EXPERTS // N_GROUPS
TOP_K = 2
D_EXPERT = 512
ALPHA = (2 * DEPTH) ** 0.25
BETA = (8 * DEPTH) ** -0.25
LN_EPS = 1e-5
RMS_EPS = 1e-6

kernel_name = "hybrid_rglru_axial_gqa_grouped_moe_encoder"


def layer_norm(x, g, b):
    xf = x.astype(jnp.float32)
    mu = jnp.mean(xf, axis=-1, keepdims=True)
    var = jnp.mean(jnp.square(xf - mu), axis=-1, keepdims=True)
    y = (xf - mu) * lax.rsqrt(var + LN_EPS) * g.astype(jnp.float32) + b.astype(jnp.float32)
    return y.astype(x.dtype)


def rms_norm_f32(x, g):
    xf = x.astype(jnp.float32)
    return xf * lax.rsqrt(jnp.mean(jnp.square(xf), axis=-1, keepdims=True) + RMS_EPS) * g.astype(jnp.float32)


def depthwise_conv(x, w, b):
    s = x.shape[1]
    xp = jnp.pad(x, ((0, 0), (CONV_LEFT, CONV_W - 1 - CONV_LEFT), (0, 0)))
    return sum(xp[:, k:k + s] * w[k] for k in range(CONV_W)) + b


def _combine(c1, c2):
    a1, b1 = c1
    a2, b2 = c2
    return a1 * a2, a2 * b1 + b2


def linear_recurrence(a, u):
    _, h = lax.associative_scan(_combine, (a, u), axis=1)
    return h


def rg_lru(x, w_a, b_a, w_i, b_i, lam):
    b, s, _ = x.shape
    xb = x.reshape(b, s, LRU_BLOCKS, LRU_BW)
    r = jax.nn.sigmoid(jnp.einsum('bshi,hij->bshj', xb, w_a.astype(jnp.float32)).reshape(b, s, D_RNN) + b_a.astype(jnp.float32))
    i = jax.nn.sigmoid(jnp.einsum('bshi,hij->bshj', xb, w_i.astype(jnp.float32)).reshape(b, s, D_RNN) + b_i.astype(jnp.float32))
    log_a = -LRU_C * r * jax.nn.softplus(-lam.astype(jnp.float32))
    a = jnp.exp(log_a)
    mult = jnp.sqrt(-jnp.expm1(2.0 * log_a))
    return linear_recurrence(a, mult * (i * x))


def recurrent_block(x, w_in, conv_w, conv_b, w_a, b_a, w_i, b_i, lam, w_out):
    u = x @ w_in
    gate, rec = jnp.split(u, 2, axis=-1)
    gate = jax.nn.gelu(gate)
    c = depthwise_conv(rec, conv_w, conv_b).astype(jnp.float32)
    h_fwd = rg_lru(c, w_a[0], b_a[0], w_i[0], b_i[0], lam[0])
    h_bwd = jnp.flip(rg_lru(jnp.flip(c, axis=1), w_a[1], b_a[1], w_i[1], b_i[1], lam[1]), axis=1)
    h = (h_fwd + h_bwd).astype(x.dtype)
    return (h * gate) @ w_out


def axial_rope_tables(seq):
    n_rows = seq // GRID_W
    row = jnp.repeat(jnp.arange(n_rows), GRID_W).astype(jnp.float32)
    col = jnp.tile(jnp.arange(GRID_W), n_rows).astype(jnp.float32)
    freqs = ROPE_THETA ** (-jnp.arange(ROPE_PAIRS, dtype=jnp.float32) / ROPE_PAIRS)
    ang = jnp.stack([row[:, None] * freqs, col[:, None] * freqs], axis=1)
    return jnp.cos(ang), jnp.sin(ang)


def apply_axial_rope(x, cos, sin):
    xs = x.reshape(*x.shape[:-1], 2, 2, ROPE_PAIRS)
    x1 = xs[..., 0, :]
    x2 = xs[..., 1, :]
    c = cos[:, None]
    s = sin[:, None]
    out = jnp.stack([x1 * c - x2 * s, x2 * c + x1 * s], axis=-2)
    return out.reshape(x.shape)


def attention_block(x, cos, sin, w_qkv, q_gain, k_gain, w_o):
    b, s, _ = x.shape
    qkv = x @ w_qkv
    q, k, v = jnp.split(qkv, [N_Q_HEADS * HEAD_DIM, (N_Q_HEADS + N_KV_HEADS) * HEAD_DIM], axis=-1)
    q = q.reshape(b, s, N_Q_HEADS, HEAD_DIM)
    k = k.reshape(b, s, N_KV_HEADS, HEAD_DIM)
    v = v.reshape(b, s, N_KV_HEADS, HEAD_DIM)
    q = apply_axial_rope(rms_norm_f32(q, q_gain), cos, sin).astype(x.dtype)
    k = apply_axial_rope(rms_norm_f32(k, k_gain), cos, sin).astype(x.dtype)
    qb = q.reshape(b, s // Q_BLOCK, Q_BLOCK, N_KV_HEADS, Q_PER_KV, HEAD_DIM).transpose(1, 0, 2, 3, 4, 5)
    scale = HEAD_DIM ** -0.5

    def one_block(q_blk):
        sc = jnp.einsum('bqkgd,bskd->bkgqs', q_blk, k).astype(jnp.float32) * scale
        p = jax.nn.softmax(sc, axis=-1).astype(v.dtype)
        return jnp.einsum('bkgqs,bskd->bqkgd', p, v)

    o = lax.map(one_block, qb)
    o = o.transpose(1, 0, 2, 3, 4, 5).reshape(b, s, N_Q_HEADS * HEAD_DIM)
    return o @ w_o


def moe_block(x, router_w, router_b, w_gate, w_up, w_down):
    b, s, d = x.shape
    n = b * s
    xt = x.reshape(n, d)
    logits = (xt @ router_w).astype(jnp.float32) + router_b.astype(jnp.float32)
    probs = jax.nn.softmax(logits, axis=-1)
    grp = probs.reshape(n, N_GROUPS, GROUP_SIZE)
    grp_score = jnp.sum(lax.top_k(grp, TOP_K)[0], axis=-1)
    sel_group = jnp.argmax(grp_score, axis=-1)
    in_group = jnp.take_along_axis(grp, sel_group[:, None, None], axis=1)[:, 0]
    top_vals, top_idx = lax.top_k(in_group, TOP_K)
    weights = top_vals / jnp.sum(top_vals, axis=-1, keepdims=True)
    expert_idx = sel_group[:, None] * GROUP_SIZE + top_idx
    gates = jnp.sum(jax.nn.one_hot(expert_idx, N_EXPERTS, dtype=jnp.float32) * weights[..., None], axis=1)
    gates_t = gates.T.astype(x.dtype)

    def expert_step(acc, p):
        wg, wu, wd, g = p
        h = jax.nn.silu(xt @ wg) * (xt @ wu)
        return acc + (h @ wd) * g[:, None], None

    acc0 = jnp.zeros((n, d), dtype=x.dtype)
    out, _ = lax.scan(expert_step, acc0, (w_gate, w_up, w_down, gates_t))
    return out.reshape(b, s, d)


def trunk(x, lru_w_in, lru_conv_w, lru_conv_b, lru_w_a, lru_b_a, lru_w_i, lru_b_i, lru_lambda, lru_w_out,
          attn_w_qkv, attn_q_gain, attn_k_gain, attn_w_o, ln1_g, ln1_b, ln2_g, ln2_b,
          router_w, router_b, moe_w_gate, moe_w_up, moe_w_down):
    cos, sin = axial_rope_tables(x.shape[1])
    for i in range(DEPTH):
        j = i // N_MIXERS
        if i % N_MIXERS == 0:
            y = recurrent_block(x, lru_w_in[j], lru_conv_w[j], lru_conv_b[j], lru_w_a[j], lru_b_a[j],
                                lru_w_i[j], lru_b_i[j], lru_lambda[j], lru_w_out[j])
        else:
            y = attention_block(x, cos, sin, attn_w_qkv[j], attn_q_gain[j], attn_k_gain[j], attn_w_o[j])
        x = layer_norm(ALPHA * x + y, ln1_g[i], ln1_b[i])
        y = moe_block(x, router_w, router_b, moe_w_gate[i], moe_w_up[i], moe_w_down[i])
        x = layer_norm(ALPHA * x + y, ln2_g[i], ln2_b[i])
    return x


def setup_inputs(seed: int = 0) -> dict:
    key = jax.random.key(seed)
    ks = jax.random.split(key, 24)
    f32 = jnp.float32
    nrm = lambda k, shape, scale: jax.random.normal(k, shape, f32) * scale
    u = jax.random.uniform(ks[9], (N_LRU_LAYERS, 2, D_RNN), f32, minval=0.9, maxval=0.999)
    p = u ** (1.0 / LRU_C)
    lam = jnp.log(p) - jnp.log1p(-p)
    return {
        "x_prompt": nrm(ks[0], (BATCH, SEQ, D_MODEL), 1.0),
        "x_sample": nrm(ks[1], (DEC_BATCH, DEC_SEQ, D_MODEL), 1.0),
        "lru_w_in": nrm(ks[2], (N_LRU_LAYERS, D_MODEL, 2 * D_RNN), D_MODEL ** -0.5),
        "lru_conv_w": nrm(ks[3], (N_LRU_LAYERS, CONV_W, D_RNN), CONV_W ** -0.5),
        "lru_conv_b": nrm(ks[4], (N_LRU_LAYERS, D_RNN), 0.01),
        "lru_w_a": nrm(ks[5], (N_LRU_LAYERS, 2, LRU_BLOCKS, LRU_BW, LRU_BW), LRU_BW ** -0.5),
        "lru_b_a": nrm(ks[6], (N_LRU_LAYERS, 2, D_RNN), 0.01),
        "lru_w_i": nrm(ks[7], (N_LRU_LAYERS, 2, LRU_BLOCKS, LRU_BW, LRU_BW), LRU_BW ** -0.5),
        "lru_b_i": nrm(ks[8], (N_LRU_LAYERS, 2, D_RNN), 0.01),
        "lru_lambda": lam,
        "lru_w_out": nrm(ks[10], (N_LRU_LAYERS, D_RNN, D_MODEL), BETA * D_RNN ** -0.5),
        "attn_w_qkv": nrm(ks[11], (N_ATTN_LAYERS, D_MODEL, (N_Q_HEADS + 2 * N_KV_HEADS) * HEAD_DIM), D_MODEL ** -0.5),
        "attn_q_gain": 1.0 + nrm(ks[12], (N_ATTN_LAYERS, HEAD_DIM), 0.02),
        "attn_k_gain": 1.0 + nrm(ks[13], (N_ATTN_LAYERS, HEAD_DIM), 0.02),
        "attn_w_o": nrm(ks[14], (N_ATTN_LAYERS, N_Q_HEADS * HEAD_DIM, D_MODEL), BETA * (N_Q_HEADS * HEAD_DIM) ** -0.5),
        "ln1_g": 1.0 + nrm(ks[15], (DEPTH, D_MODEL), 0.02),
        "ln1_b": nrm(ks[16], (DEPTH, D_MODEL), 0.02),
        "ln2_g": 1.0 + nrm(ks[17], (DEPTH, D_MODEL), 0.02),
        "ln2_b": nrm(ks[18], (DEPTH, D_MODEL), 0.02),
        "router_w": nrm(ks[19], (D_MODEL, N_EXPERTS), D_MODEL ** -0.5),
        "router_b": nrm(ks[20], (N_EXPERTS,), 0.01),
        "moe_w_gate": nrm(ks[21], (DEPTH, N_EXPERTS, D_MODEL, D_EXPERT), D_MODEL ** -0.5),
        "moe_w_up": nrm(ks[22], (DEPTH, N_EXPERTS, D_MODEL, D_EXPERT), D_MODEL ** -0.5),
        "moe_w_down": nrm(ks[23], (DEPTH, N_EXPERTS, D_EXPERT, D_MODEL), BETA * D_EXPERT ** -0.5),
    }


def reference(x_prompt, x_sample, lru_w_in, lru_conv_w, lru_conv_b, lru_w_a, lru_b_a, lru_w_i, lru_b_i,
              lru_lambda, lru_w_out, attn_w_qkv, attn_q_gain, attn_k_gain, attn_w_o, ln1_g, ln1_b,
              ln2_g, ln2_b, router_w, router_b, moe_w_gate, moe_w_up, moe_w_down):
    params = (lru_w_in, lru_conv_w, lru_conv_b, lru_w_a, lru_b_a, lru_w_i, lru_b_i, lru_lambda, lru_w_out,
              attn_w_qkv, attn_q_gain, attn_k_gain, attn_w_o, ln1_g, ln1_b, ln2_g, ln2_b,
              router_w, router_b, moe_w_gate, moe_w_up, moe_w_down)
    y_prompt = trunk(x_prompt, *params)
    y_sample = trunk(x_sample, *params)
    return (y_prompt, y_sample)
```

```python
import functools
import math

import jax
import jax.numpy as jnp
from jax import lax
from jax.experimental import pallas as pl
from jax.experimental.pallas import tpu as pltpu

F32 = jnp.float32
BF16 = jnp.bfloat16

DEPTH = 4
GRID_W = 64
LRU_BW = 128
CONV_W = 4
LRU_C = 8.0
HEAD_DIM = 128
N_Q_HEADS = 8
N_KV_HEADS = 2
Q_PER_KV = N_Q_HEADS // N_KV_HEADS
ROPE_THETA = 10000.0
ROPE_PAIRS = HEAD_DIM // 4
N_EXPERTS = 16
N_GROUPS = 4
GROUP_SIZE = N_EXPERTS // N_GROUPS
ALPHA = (2 * DEPTH) ** 0.25
LN_EPS = 1e-5
RMS_EPS = 1e-6

LANES = 128
SUBLANES = 8
VMEM_LIMIT_BYTES = 48 * 1024 * 1024


def _params(*semantics):
    return pltpu.CompilerParams(dimension_semantics=semantics,
                                vmem_limit_bytes=VMEM_LIMIT_BYTES)


def _gelu_tanh(x):
    c = math.sqrt(2.0 / math.pi)
    return x * (0.5 * (1.0 + jnp.tanh(c * (x + 0.044715 * (x * x * x)))))


def _one_minus_exp(z, exp_z):
    series = -z * (1.0 + z * (1.0 / 2 + z * (1.0 / 6 + z * (1.0 / 24 + z * (1.0 / 120)))))
    return jnp.where(z > -0.01, series, 1.0 - exp_z)


def _layer_norm(z, g, b):
    mu = jnp.mean(z, axis=-1, keepdims=True)
    zc = z - mu
    var = jnp.mean(zc * zc, axis=-1, keepdims=True)
    return zc * lax.rsqrt(var + LN_EPS) * g + b


def _lru_inproj_kernel(x_ref, w_ref, gate_ref, rec_ref):
    c = gate_ref.shape[-1]
    u = jnp.dot(x_ref[0].astype(BF16), w_ref[...], preferred_element_type=F32)
    gate_ref[0] = _gelu_tanh(u[:, :c]).astype(BF16)
    rec_ref[0] = u[:, c:]


def _lru_inproj(x, w_in):
    b, s, d = x.shape
    c = w_in.shape[1] // 2
    tm = min(512, s)
    tok = lambda bi, i: (bi, i, 0)
    return pl.pallas_call(
        _lru_inproj_kernel, grid=(b, s // tm),
        in_specs=[pl.BlockSpec((1, tm, d), tok), pl.BlockSpec((d, 2 * c), lambda bi, i: (0, 0))],
        out_specs=(pl.BlockSpec((1, tm, c), tok), pl.BlockSpec((1, tm, c), tok)),
        out_shape=(jax.ShapeDtypeStruct((b, s, c), BF16), jax.ShapeDtypeStruct((b, s, c), F32)),
        compiler_params=_params("parallel", "parallel"), name="lru_inproj")(x, w_in)


def _lru_scan_kernel(cur_f, prev_f, next_f, cur_b, prev_b, next_b, cw_ref, vec_ref, wg_ref,
                     hf_ref, hb_ref, ext_s, a_s, u_s, hc_s, *, n_chunks):
    i = pl.program_id(1)
    bb, t_len, c_dim = cur_f.shape
    halo = prev_f.shape[1]

    @pl.when(i == 0)
    def _():
        hc_s[...] = jnp.zeros_like(hc_s)

    def gates(cur, prev, nxt, bi, is_first, is_last, d):
        ext_s[pl.ds(0, halo), :] = jnp.where(is_first, 0.0, prev[bi])
        ext_s[pl.ds(halo, t_len), :] = cur[bi]
        ext_s[pl.ds(halo + t_len, halo), :] = jnp.where(is_last, 0.0, nxt[bi])
        c = cw_ref[0:1, :] * ext_s[pl.ds(halo - 1, t_len), :]
        for k in range(1, CONV_W):
            c = c + cw_ref[k:k + 1, :] * ext_s[pl.ds(halo - 1 + k, t_len), :]
        c = c + vec_ref[6:7, :]
        idx = 2 * bi + d
        for kk in range(c_dim // LRU_BW):
            sl = slice(kk * LRU_BW, (kk + 1) * LRU_BW)
            ck = c[:, sl]
            z = jnp.dot(ck.astype(BF16), wg_ref[d, kk], preferred_element_type=F32)
            r = jax.nn.sigmoid(z[:, :LRU_BW] + vec_ref[3 * d:3 * d + 1, sl])
            ig = jax.nn.sigmoid(z[:, LRU_BW:] + vec_ref[3 * d + 1:3 * d + 2, sl])
            lam = vec_ref[3 * d + 2:3 * d + 3, sl]
            softplus = jnp.maximum(-lam, 0.0) + jnp.log1p(jnp.exp(-jnp.abs(lam)))
            log_a = (-LRU_C) * r * softplus
            a = jnp.exp(log_a)
            a_s[idx, :, sl] = a
            u_s[idx, :, sl] = jnp.sqrt(_one_minus_exp(2.0 * log_a, a * a)) * (ig * ck)

    for bi in range(bb):
        gates(cur_f, prev_f, next_f, bi, i == 0, i == n_chunks - 1, 0)
        gates(cur_b, prev_b, next_b, bi, i == n_chunks - 1, i == 0, 1)

    def step(t, hs):
        new = []
        for bi in range(bb):
            for d, out in ((0, hf_ref), (1, hb_ref)):
                idx = 2 * bi + d
                tt = t if d == 0 else t_len - 1 - t
                h = a_s[idx, pl.ds(tt, 1), :] * hs[idx] + u_s[idx, pl.ds(tt, 1), :]
                out[bi, pl.ds(tt, 1), :] = h
                new.append(h)
        return tuple(new)

    init = tuple(hc_s[idx, pl.ds(0, 1), :] for idx in range(2 * bb))
    hs = lax.fori_loop(0, t_len, step, init, unroll=8)
    for idx in range(2 * bb):
        hc_s[idx, pl.ds(0, 1), :] = hs[idx]


def _lru_scan(rec, conv_w, conv_b, w_a, b_a, w_i, b_i, lam):
    b, s, c = rec.shape
    t_len = min(128, s)
    bb = 2
    halo = SUBLANES
    n_chunks = s // t_len
    n_halo_blocks = s // halo
    per_chunk = t_len // halo
    cw = jnp.zeros((SUBLANES, c), F32).at[:CONV_W].set(conv_w)
    vec = jnp.zeros((SUBLANES, c), F32)
    vec = vec.at[0].set(b_a[0]).at[1].set(b_i[0]).at[2].set(lam[0])
    vec = vec.at[3].set(b_a[1]).at[4].set(b_i[1]).at[5].set(lam[1]).at[6].set(conv_b)
    wg = jnp.concatenate([w_a, w_i], axis=-1).astype(BF16)

    def chunk(rev, i):
        return (n_chunks - 1 - i) if rev else i

    def specs(rev):
        return [
            pl.BlockSpec((bb, t_len, c), lambda g, i: (g, chunk(rev, i), 0)),
            pl.BlockSpec((bb, halo, c),
                         lambda g, i: (g, jnp.maximum(chunk(rev, i) * per_chunk - 1, 0), 0)),
            pl.BlockSpec((bb, halo, c),
                         lambda g, i: (g, jnp.minimum((chunk(rev, i) + 1) * per_chunk,
                                                      n_halo_blocks - 1), 0))]

    const = lambda g, i: (0, 0)
    in_specs = specs(False) + specs(True) + [
        pl.BlockSpec((SUBLANES, c), const), pl.BlockSpec((SUBLANES, c), const),
        pl.BlockSpec(wg.shape, lambda g, i: (0, 0, 0, 0))]
    out_specs = (pl.BlockSpec((bb, t_len, c), lambda g, i: (g, i, 0)),
                 pl.BlockSpec((bb, t_len, c), lambda g, i: (g, n_chunks - 1 - i, 0)))
    scratch = [pltpu.VMEM((t_len + 2 * halo, c), F32),
               pltpu.VMEM((2 * bb, t_len, c), F32), pltpu.VMEM((2 * bb, t_len, c), F32),
               pltpu.VMEM((2 * bb, SUBLANES, c), F32)]
    return pl.pallas_call(
        functools.partial(_lru_scan_kernel, n_chunks=n_chunks),
        grid=(b // bb, n_chunks), in_specs=in_specs, out_specs=out_specs,
        out_shape=(jax.ShapeDtypeStruct((b, s, c), F32),) * 2, scratch_shapes=scratch,
        compiler_params=_params("parallel", "arbitrary"), name="lru_scan",
    )(rec, rec, rec, rec, rec, rec, cw, vec, wg)


def _route(logits_t):
    m = jnp.max(logits_t, axis=0, keepdims=True)
    e = jnp.exp(logits_t - m)
    probs = e / jnp.sum(e, axis=0, keepdims=True)
    p = [probs[k:k + 1, :] for k in range(N_EXPERTS)]

    scores = []
    for g in range(N_GROUPS):
        a, b, c, d = p[GROUP_SIZE * g:GROUP_SIZE * (g + 1)]
        top2 = jnp.maximum(jnp.maximum(a + b, a + c), jnp.maximum(a + d, b + c))
        scores.append(jnp.maximum(top2, jnp.maximum(b + d, c + d)))
    best = scores[0]
    sel = jnp.zeros(best.shape, jnp.int32)
    for g in range(1, N_GROUPS):
        better = scores[g] > best
        sel = jnp.where(better, g, sel)
        best = jnp.where(better, scores[g], best)

    v = list(p[:GROUP_SIZE])
    for g in range(1, N_GROUPS):
        for k in range(GROUP_SIZE):
            v[k] = jnp.where(sel == g, p[GROUP_SIZE * g + k], v[k])
    t1 = v[0]
    i1 = jnp.zeros(best.shape, jnp.int32)
    for k in range(1, GROUP_SIZE):
        better = v[k] > t1
        i1 = jnp.where(better, k, i1)
        t1 = jnp.where(better, v[k], t1)
    t2 = jnp.full(best.shape, -jnp.inf, F32)
    i2 = jnp.zeros(best.shape, jnp.int32)
    for k in range(GROUP_SIZE):
        better = (i1 != k) & (v[k] > t2)
        i2 = jnp.where(better, k, i2)
        t2 = jnp.where(better, v[k], t2)
    denom = t1 + t2
    w1 = t1 / denom
    w2 = t2 / denom

    rows = []
    for g in range(N_GROUPS):
        for k in range(GROUP_SIZE):
            w = jnp.where(i1 == k, w1, 0.0) + jnp.where(i2 == k, w2, 0.0)
            rows.append(jnp.where(sel == g, w, 0.0))
    return jnp.concatenate(rows, axis=0)


def _mixer_tail(a, x, w_ref, lng_ref, lnb_ref, rwh_ref, rwl_ref, rb_ref, x1_ref, gates_ref):
    y = jnp.dot(a, w_ref[...], preferred_element_type=F32)
    x1 = _layer_norm(ALPHA * x + y, lng_ref[...], lnb_ref[...])
    x1_ref[0] = x1
    hi = x1.astype(BF16)
    lo = (x1 - hi.astype(F32)).astype(BF16)
    logits = (jnp.dot(hi, rwh_ref[...], preferred_element_type=F32)
              + jnp.dot(lo, rwh_ref[...], preferred_element_type=F32)
              + jnp.dot(hi, rwl_ref[...], preferred_element_type=F32)) + rb_ref[...]
    gates_t = _route(logits.T[:N_EXPERTS])
    pad = jnp.zeros((LANES - N_EXPERTS, gates_t.shape[1]), F32)
    gates_ref[0] = jnp.concatenate([gates_t, pad], axis=0).T


def _lru_tail_kernel(hf_ref, hb_ref, gate_ref, x_ref, *rest):
    a = ((hf_ref[0] + hb_ref[0]) * gate_ref[0].astype(F32)).astype(BF16)
    _mixer_tail(a, x_ref[0], *rest)


def _attn_tail_kernel(o_ref, x_ref, *rest):
    _mixer_tail(o_ref[0], x_ref[0], *rest)


def _mixer_tail_call(kern, name, acts, act_specs, x, w, ln_g, ln_b, router):
    b, s, d = x.shape
    tm = min(512, s)
    rwh, rwl, rb = router
    c = w.shape[0]
    tok = lambda bi, i: (bi, i, 0)
    const = lambda bi, i: (0, 0)
    in_specs = act_specs(tm) + [
        pl.BlockSpec((1, tm, d), tok),
        pl.BlockSpec((c, d), const),
        pl.BlockSpec((1, d), const), pl.BlockSpec((1, d), const),
        pl.BlockSpec((d, LANES), const), pl.BlockSpec((d, LANES), const),
        pl.BlockSpec((1, LANES), const)]
    out_specs = (pl.BlockSpec((1, tm, d), tok), pl.BlockSpec((1, tm, LANES), tok))
    out_shape = (jax.ShapeDtypeStruct((b, s, d), F32),
                 jax.ShapeDtypeStruct((b, s, LANES), F32))
    return pl.pallas_call(
        kern, grid=(b, s // tm), in_specs=in_specs, out_specs=out_specs, out_shape=out_shape,
        compiler_params=_params("parallel", "parallel"), name=name,
    )(*acts, x, w, ln_g.reshape(1, d), ln_b.reshape(1, d), rwh, rwl, rb)


def _lru_tail(hf, hb, gate, x, w_out, ln_g, ln_b, router):
    c = gate.shape[-1]
    specs = lambda tm: [pl.BlockSpec((1, tm, c), lambda bi, i: (bi, i, 0))] * 3
    return _mixer_tail_call(_lru_tail_kernel, "lru_tail", (hf, hb, gate), specs,
                            x, w_out, ln_g, ln_b, router)


def _attn_tail(o, x, w_o, ln_g, ln_b, router):
    c = o.shape[-1]
    specs = lambda tm: [pl.BlockSpec((1, tm, c), lambda bi, i: (bi, i, 0))]
    return _mixer_tail_call(_attn_tail_kernel, "attn_tail", (o,), specs,
                            x, w_o, ln_g, ln_b, router)


def _qkv_kernel(x_ref, w_ref, qg_ref, kg_ref, cos_ref, sin_ref, q_ref, k_ref, v_ref):
    u = jnp.dot(x_ref[0].astype(BF16), w_ref[...], preferred_element_type=F32)
    cos = cos_ref[...]
    sin = sin_ref[...]
    lane = lax.broadcasted_iota(jnp.int32, cos.shape, 1)
    first_half = (lane % (2 * ROPE_PAIRS)) < ROPE_PAIRS

    def norm_rope(xh, gain):
        ms = jnp.mean(xh * xh, axis=-1, keepdims=True)
        xn = xh * lax.rsqrt(ms + RMS_EPS) * gain
        partner = jnp.where(first_half,
                            pltpu.roll(xn, HEAD_DIM - ROPE_PAIRS, axis=1),
                            pltpu.roll(xn, ROPE_PAIRS, axis=1))
        return (xn * cos + partner * sin).astype(BF16)

    nq = N_Q_HEADS * HEAD_DIM
    nk = N_KV_HEADS * HEAD_DIM
    for h in range(N_Q_HEADS):
        sl = slice(h * HEAD_DIM, (h + 1) * HEAD_DIM)
        q_ref[0, :, sl] = norm_rope(u[:, sl], qg_ref[...])
    for h in range(N_KV_HEADS):
        sl = slice(h * HEAD_DIM, (h + 1) * HEAD_DIM)
        k_ref[0, :, sl] = norm_rope(u[:, nq + h * HEAD_DIM:nq + (h + 1) * HEAD_DIM], kg_ref[...])
    v_ref[0] = u[:, nq + nk:].astype(BF16)


def _rope_tables(s):
    pos = jnp.arange(s)
    row = (pos // GRID_W).astype(F32)
    col = (pos % GRID_W).astype(F32)
    freqs = ROPE_THETA ** (-jnp.arange(ROPE_PAIRS, dtype=F32) / ROPE_PAIRS)
    ang_r = row[:, None] * freqs
    ang_c = col[:, None] * freqs
    cos = jnp.concatenate([jnp.cos(ang_r)] * 2 + [jnp.cos(ang_c)] * 2, axis=1)
    sin = jnp.concatenate([-jnp.sin(ang_r), jnp.sin(ang_r), -jnp.sin(ang_c), jnp.sin(ang_c)], axis=1)
    return cos, sin


def _qkv_proj(x, w_qkv, q_gain, k_gain, cos, sin):
    b, s, d = x.shape
    tm = min(512, s)
    nq = N_Q_HEADS * HEAD_DIM
    nk = N_KV_HEADS * HEAD_DIM
    tok = lambda bi, i: (bi, i, 0)
    const = lambda bi, i: (0, 0)
    in_specs = [pl.BlockSpec((1, tm, d), tok),
                pl.BlockSpec((d, nq + 2 * nk), const),
                pl.BlockSpec((1, HEAD_DIM), const), pl.BlockSpec((1, HEAD_DIM), const),
                pl.BlockSpec((tm, HEAD_DIM), lambda bi, i: (i, 0)),
                pl.BlockSpec((tm, HEAD_DIM), lambda bi, i: (i, 0))]
    out_specs = (pl.BlockSpec((1, tm, nq), tok), pl.BlockSpec((1, tm, nk), tok),
                 pl.BlockSpec((1, tm, nk), tok))
    out_shape = (jax.ShapeDtypeStruct((b, s, nq), BF16),
                 jax.ShapeDtypeStruct((b, s, nk), BF16),
                 jax.ShapeDtypeStruct((b, s, nk), BF16))
    return pl.pallas_call(
        _qkv_kernel, grid=(b, s // tm), in_specs=in_specs, out_specs=out_specs,
        out_shape=out_shape, compiler_params=_params("parallel", "parallel"), name="qkv_proj",
    )(x, w_qkv, q_gain.reshape(1, HEAD_DIM), k_gain.reshape(1, HEAD_DIM), cos, sin)


def _flash_kernel(q_ref, k_ref, v_ref, o_ref, *, tkv):
    tq = q_ref.shape[1]
    s_len = k_ref.shape[1]
    scale = HEAD_DIM ** -0.5
    q = jnp.concatenate([q_ref[0, :, h * HEAD_DIM:(h + 1) * HEAD_DIM] for h in range(Q_PER_KV)],
                        axis=0)
    rows = q.shape[0]

    def body(c, carry):
        m, l, acc = carry
        start = pl.multiple_of(c * tkv, tkv)
        kc = k_ref[0, pl.ds(start, tkv), :]
        vc = v_ref[0, pl.ds(start, tkv), :]
        s = lax.dot_general(q, kc, (((1,), (1,)), ((), ())), preferred_element_type=F32) * scale
        m_new = jnp.maximum(m, jnp.max(s, axis=-1, keepdims=True))
        p = jnp.exp(s - m_new)
        alpha = jnp.exp(m - m_new)
        l = alpha * l + jnp.sum(p, axis=-1, keepdims=True)
        acc = alpha * acc + jnp.dot(p.astype(BF16), vc, preferred_element_type=F32)
        return m_new, l, acc

    init = (jnp.full((rows, 1), -jnp.inf, F32), jnp.zeros((rows, 1), F32),
            jnp.zeros((rows, HEAD_DIM), F32))
    _, l, acc = lax.fori_loop(0, s_len // tkv, body, init)
    o = (acc / l).astype(BF16)
    for h in range(Q_PER_KV):
        o_ref[0, :, h * HEAD_DIM:(h + 1) * HEAD_DIM] = o[h * tq:(h + 1) * tq]


def _flash_attention(q, k, v):
    b, s, nq = q.shape
    tq = min(256, s)
    tkv = min(512, s)
    gw = Q_PER_KV * HEAD_DIM
    in_specs = [pl.BlockSpec((1, tq, gw), lambda bi, g, i: (bi, i, g)),
                pl.BlockSpec((1, s, HEAD_DIM), lambda bi, g, i: (bi, 0, g)),
                pl.BlockSpec((1, s, HEAD_DIM), lambda bi, g, i: (bi, 0, g))]
    out_specs = pl.BlockSpec((1, tq, gw), lambda bi, g, i: (bi, i, g))
    return pl.pallas_call(
        functools.partial(_flash_kernel, tkv=tkv), grid=(b, N_KV_HEADS, s // tq),
        in_specs=in_specs, out_specs=out_specs,
        out_shape=jax.ShapeDtypeStruct((b, s, nq), BF16),
        compiler_params=_params("parallel", "parallel", "parallel"), name="flash_attn",
    )(q, k, v)


def _moe_kernel(x_ref, gates_ref, wg_ref, wu_ref, wd_ref, lng_ref, lnb_ref, o_ref, xb_s, acc_s):
    e = pl.program_id(1)

    @pl.when(e == 0)
    def _():
        xb_s[...] = x_ref[...].astype(BF16)
        acc_s[...] = jnp.zeros_like(acc_s)

    xb = xb_s[...]
    hg = jnp.dot(xb, wg_ref[0], preferred_element_type=F32)
    hu = jnp.dot(xb, wu_ref[0], preferred_element_type=F32)
    h = (hg * jax.nn.sigmoid(hg)) * hu
    y = jnp.dot(h.astype(BF16), wd_ref[0], preferred_element_type=F32)
    gates = gates_ref[...]
    lane = lax.broadcasted_iota(jnp.int32, gates.shape, 1)
    g = jnp.sum(jnp.where(lane == e, gates, 0.0), axis=-1, keepdims=True)
    acc_s[...] += y * g

    @pl.when(e == pl.num_programs(1) - 1)
    def _():
        o_ref[...] = _layer_norm(ALPHA * x_ref[...] + acc_s[...], lng_ref[...], lnb_ref[...])


def _moe(x1, gates, w_gate, w_up, w_down, ln_g, ln_b):
    b, s, d = x1.shape
    n = b * s
    f = w_gate.shape[-1]
    tm = min(512, n)
    x2 = x1.reshape(n, d)
    g2 = gates.reshape(n, LANES)
    in_specs = [pl.BlockSpec((tm, d), lambda i, e: (i, 0)),
                pl.BlockSpec((tm, LANES), lambda i, e: (i, 0)),
                pl.BlockSpec((1, d, f), lambda i, e: (e, 0, 0)),
                pl.BlockSpec((1, d, f), lambda i, e: (e, 0, 0)),
                pl.BlockSpec((1, f, d), lambda i, e: (e, 0, 0)),
                pl.BlockSpec((1, d), lambda i, e: (0, 0)),
                pl.BlockSpec((1, d), lambda i, e: (0, 0))]
    out = pl.pallas_call(
        _moe_kernel, grid=(n // tm, N_EXPERTS), in_specs=in_specs,
        out_specs=pl.BlockSpec((tm, d), lambda i, e: (i, 0)),
        out_shape=jax.ShapeDtypeStruct((n, d), F32),
        scratch_shapes=[pltpu.VMEM((tm, d), BF16), pltpu.VMEM((tm, d), F32)],
        compiler_params=_params("parallel", "arbitrary"), name="moe",
    )(x2, g2, w_gate, w_up, w_down, ln_g.reshape(1, d), ln_b.reshape(1, d))
    return out.reshape(b, s, d)


def _trunk(x, p):
    b, s, d = x.shape
    cos, sin = _rope_tables(s)
    for i in range(DEPTH):
        j = i // 2
        if i % 2 == 0:
            gate, rec = _lru_inproj(x, p["lru_w_in"][j])
            hf, hb = _lru_scan(rec, p["lru_conv_w"][j], p["lru_conv_b"][j], p["lru_w_a"][j],
                               p["lru_b_a"][j], p["lru_w_i"][j], p["lru_b_i"][j],
                               p["lru_lambda"][j])
            x1, gates = _lru_tail(hf, hb, gate, x, p["lru_w_out"][j], p["ln1_g"][i],
                                  p["ln1_b"][i], p["router"])
        else:
            q, k, v = _qkv_proj(x, p["attn_w_qkv"][j], p["attn_q_gain"][j], p["attn_k_gain"][j],
                                cos, sin)
            o = _flash_attention(q, k, v)
            x1, gates = _attn_tail(o, x, p["attn_w_o"][j], p["ln1_g"][i], p["ln1_b"][i],
                                   p["router"])
        x = _moe(x1, gates, p["moe_w_gate"][i], p["moe_w_up"][i], p["moe_w_down"][i],
                 p["ln2_g"][i], p["ln2_b"][i])
    return x


def kernel(x_prompt, x_sample, lru_w_in, lru_conv_w, lru_conv_b, lru_w_a, lru_b_a, lru_w_i, lru_b_i,
           lru_lambda, lru_w_out, attn_w_qkv, attn_q_gain, attn_k_gain, attn_w_o, ln1_g, ln1_b,
           ln2_g, ln2_b, router_w, router_b, moe_w_gate, moe_w_up, moe_w_down):
    d = router_w.shape[0]
    rw = jnp.zeros((d, LANES), F32).at[:, :N_EXPERTS].set(router_w)
    rw_hi = rw.astype(BF16)
    rw_lo = (rw - rw_hi.astype(F32)).astype(BF16)
    rb = jnp.zeros((1, LANES), F32).at[0, :N_EXPERTS].set(router_b)
    p = dict(
        lru_w_in=lru_w_in.astype(BF16), lru_conv_w=lru_conv_w, lru_conv_b=lru_conv_b,
        lru_w_a=lru_w_a, lru_b_a=lru_b_a, lru_w_i=lru_w_i, lru_b_i=lru_b_i, lru_lambda=lru_lambda,
        lru_w_out=lru_w_out.astype(BF16), attn_w_qkv=attn_w_qkv.astype(BF16),
        attn_q_gain=attn_q_gain, attn_k_gain=attn_k_gain, attn_w_o=attn_w_o.astype(BF16),
        ln1_g=ln1_g, ln1_b=ln1_b, ln2_g=ln2_g, ln2_b=ln2_b, router=(rw_hi, rw_lo, rb),
        moe_w_gate=moe_w_gate.astype(BF16), moe_w_up=moe_w_up.astype(BF16),
        moe_w_down=moe_w_down.astype(BF16))
    return _trunk(x_prompt, p), _trunk(x_sample, p)
```

```python
import functools
import math

import jax
import jax.numpy as jnp
from jax import lax
from jax.experimental import pallas as pl
from jax.experimental.pallas import tpu as pltpu

F32 = jnp.float32
BF16 = jnp.bfloat16

DEPTH = 4
GRID_W = 64
LRU_BW = 128
CONV_W = 4
LRU_C = 8.0
HEAD_DIM = 128
N_Q_HEADS = 8
N_KV_HEADS = 2
Q_PER_KV = N_Q_HEADS // N_KV_HEADS
ROPE_THETA = 10000.0
ROPE_PAIRS = HEAD_DIM // 4
N_EXPERTS = 16
N_GROUPS = 4
GROUP_SIZE = N_EXPERTS // N_GROUPS
Q_SCALE = HEAD_DIM ** -0.5 * math.log2(math.e)
ALPHA = (2 * DEPTH) ** 0.25
LN_EPS = 1e-5
RMS_EPS = 1e-6

LANES = 128
SUBLANES = 8
VMEM_LIMIT_BYTES = 48 * 1024 * 1024


def _params(*semantics):
    return pltpu.CompilerParams(dimension_semantics=semantics,
                                vmem_limit_bytes=VMEM_LIMIT_BYTES)


def _gelu_tanh(x):
    c = math.sqrt(2.0 / math.pi)
    return x * (0.5 * (1.0 + jnp.tanh(c * (x + 0.044715 * (x * x * x)))))


def _one_minus_exp(z, exp_z):
    series = -z * (1.0 + z * (1.0 / 2 + z * (1.0 / 6 + z * (1.0 / 24 + z * (1.0 / 120)))))
    return jnp.where(z > -0.01, series, 1.0 - exp_z)


def _layer_norm(z, g, b):
    mu = jnp.mean(z, axis=-1, keepdims=True)
    zc = z - mu
    var = jnp.mean(zc * zc, axis=-1, keepdims=True)
    return zc * lax.rsqrt(var + LN_EPS) * g + b


def _lru_inproj_kernel(x_ref, w_ref, gate_ref, rec_ref):
    c = gate_ref.shape[-1]
    u = jnp.dot(x_ref[0].astype(BF16), w_ref[...], preferred_element_type=F32)
    gate_ref[0] = _gelu_tanh(u[:, :c]).astype(BF16)
    rec_ref[0] = u[:, c:]


def _lru_inproj(x, w_in):
    b, s, d = x.shape
    c = w_in.shape[1] // 2
    tm = min(512, s)
    tok = lambda bi, i: (bi, i, 0)
    return pl.pallas_call(
        _lru_inproj_kernel, grid=(b, s // tm),
        in_specs=[pl.BlockSpec((1, tm, d), tok), pl.BlockSpec((d, 2 * c), lambda bi, i: (0, 0))],
        out_specs=(pl.BlockSpec((1, tm, c), tok), pl.BlockSpec((1, tm, c), tok)),
        out_shape=(jax.ShapeDtypeStruct((b, s, c), BF16), jax.ShapeDtypeStruct((b, s, c), F32)),
        compiler_params=_params("parallel", "parallel"), name="lru_inproj")(x, w_in)


def _lru_scan_kernel(cur_f, prev_f, next_f, cur_b, prev_b, next_b, cw_ref, vec_ref, wg_ref,
                     hf_ref, hb_ref, ext_s, a_s, u_s, hc_s, *, n_chunks):
    i = pl.program_id(1)
    bb, t_len, c_dim = cur_f.shape
    halo = prev_f.shape[1]

    @pl.when(i == 0)
    def _():
        hc_s[...] = jnp.zeros_like(hc_s)

    def gates(cur, prev, nxt, bi, is_first, is_last, d):
        ext_s[pl.ds(0, halo), :] = jnp.where(is_first, 0.0, prev[bi])
        ext_s[pl.ds(halo, t_len), :] = cur[bi]
        ext_s[pl.ds(halo + t_len, halo), :] = jnp.where(is_last, 0.0, nxt[bi])
        c = cw_ref[0:1, :] * ext_s[pl.ds(halo - 1, t_len), :]
        for k in range(1, CONV_W):
            c = c + cw_ref[k:k + 1, :] * ext_s[pl.ds(halo - 1 + k, t_len), :]
        c = c + vec_ref[6:7, :]
        idx = 2 * bi + d
        for kk in range(c_dim // LRU_BW):
            sl = slice(kk * LRU_BW, (kk + 1) * LRU_BW)
            ck = c[:, sl]
            z = jnp.dot(ck.astype(BF16), wg_ref[d, kk], preferred_element_type=F32)
            r = jax.nn.sigmoid(z[:, :LRU_BW] + vec_ref[3 * d:3 * d + 1, sl])
            ig = jax.nn.sigmoid(z[:, LRU_BW:] + vec_ref[3 * d + 1:3 * d + 2, sl])
            lam = vec_ref[3 * d + 2:3 * d + 3, sl]
            softplus = jnp.maximum(-lam, 0.0) + jnp.log1p(jnp.exp(-jnp.abs(lam)))
            log_a = (-LRU_C) * r * softplus
            a = jnp.exp(log_a)
            a_s[idx, :, sl] = a
            u_s[idx, :, sl] = jnp.sqrt(_one_minus_exp(2.0 * log_a, a * a)) * (ig * ck)

    for bi in range(bb):
        gates(cur_f, prev_f, next_f, bi, i == 0, i == n_chunks - 1, 0)
        gates(cur_b, prev_b, next_b, bi, i == n_chunks - 1, i == 0, 1)

    def step(t, hs):
        new = []
        for bi in range(bb):
            for d, out in ((0, hf_ref), (1, hb_ref)):
                idx = 2 * bi + d
                tt = t if d == 0 else t_len - 1 - t
                h = a_s[idx, pl.ds(tt, 1), :] * hs[idx] + u_s[idx, pl.ds(tt, 1), :]
                out[bi, pl.ds(tt, 1), :] = h
                new.append(h)
        return tuple(new)

    init = tuple(hc_s[idx, pl.ds(0, 1), :] for idx in range(2 * bb))
    hs = lax.fori_loop(0, t_len, step, init, unroll=8)
    for idx in range(2 * bb):
        hc_s[idx, pl.ds(0, 1), :] = hs[idx]


def _lru_scan(rec, conv_w, conv_b, w_a, b_a, w_i, b_i, lam):
    b, s, c = rec.shape
    t_len = min(128, s)
    bb = 2
    halo = SUBLANES
    n_chunks = s // t_len
    n_halo_blocks = s // halo
    per_chunk = t_len // halo
    cw = jnp.zeros((SUBLANES, c), F32).at[:CONV_W].set(conv_w)
    vec = jnp.zeros((SUBLANES, c), F32)
    vec = vec.at[0].set(b_a[0]).at[1].set(b_i[0]).at[2].set(lam[0])
    vec = vec.at[3].set(b_a[1]).at[4].set(b_i[1]).at[5].set(lam[1]).at[6].set(conv_b)
    wg = jnp.concatenate([w_a, w_i], axis=-1).astype(BF16)

    def chunk(rev, i):
        return (n_chunks - 1 - i) if rev else i

    def specs(rev):
        return [
            pl.BlockSpec((bb, t_len, c), lambda g, i: (g, chunk(rev, i), 0)),
            pl.BlockSpec((bb, halo, c),
                         lambda g, i: (g, jnp.maximum(chunk(rev, i) * per_chunk - 1, 0), 0)),
            pl.BlockSpec((bb, halo, c),
                         lambda g, i: (g, jnp.minimum((chunk(rev, i) + 1) * per_chunk,
                                                      n_halo_blocks - 1), 0))]

    const = lambda g, i: (0, 0)
    in_specs = specs(False) + specs(True) + [
        pl.BlockSpec((SUBLANES, c), const), pl.BlockSpec((SUBLANES, c), const),
        pl.BlockSpec(wg.shape, lambda g, i: (0, 0, 0, 0))]
    out_specs = (pl.BlockSpec((bb, t_len, c), lambda g, i: (g, i, 0)),
                 pl.BlockSpec((bb, t_len, c), lambda g, i: (g, n_chunks - 1 - i, 0)))
    scratch = [pltpu.VMEM((t_len + 2 * halo, c), F32),
               pltpu.VMEM((2 * bb, t_len, c), F32), pltpu.VMEM((2 * bb, t_len, c), F32),
               pltpu.VMEM((2 * bb, SUBLANES, c), F32)]
    return pl.pallas_call(
        functools.partial(_lru_scan_kernel, n_chunks=n_chunks),
        grid=(b // bb, n_chunks), in_specs=in_specs, out_specs=out_specs,
        out_shape=(jax.ShapeDtypeStruct((b, s, c), F32),) * 2, scratch_shapes=scratch,
        compiler_params=_params("parallel", "arbitrary"), name="lru_scan",
    )(rec, rec, rec, rec, rec, rec, cw, vec, wg)


def _route(logits_t):
    m = jnp.max(logits_t, axis=0, keepdims=True)
    e = jnp.exp(logits_t - m)
    probs = e / jnp.sum(e, axis=0, keepdims=True)
    p = [probs[k:k + 1, :] for k in range(N_EXPERTS)]

    scores = []
    for g in range(N_GROUPS):
        a, b, c, d = p[GROUP_SIZE * g:GROUP_SIZE * (g + 1)]
        top2 = jnp.maximum(jnp.maximum(a + b, a + c), jnp.maximum(a + d, b + c))
        scores.append(jnp.maximum(top2, jnp.maximum(b + d, c + d)))
    best = scores[0]
    sel = jnp.zeros(best.shape, jnp.int32)
    for g in range(1, N_GROUPS):
        better = scores[g] > best
        sel = jnp.where(better, g, sel)
        best = jnp.where(better, scores[g], best)

    v = list(p[:GROUP_SIZE])
    for g in range(1, N_GROUPS):
        for k in range(GROUP_SIZE):
            v[k] = jnp.where(sel == g, p[GROUP_SIZE * g + k], v[k])
    t1 = v[0]
    i1 = jnp.zeros(best.shape, jnp.int32)
    for k in range(1, GROUP_SIZE):
        better = v[k] > t1
        i1 = jnp.where(better, k, i1)
        t1 = jnp.where(better, v[k], t1)
    t2 = jnp.full(best.shape, -jnp.inf, F32)
    i2 = jnp.zeros(best.shape, jnp.int32)
    for k in range(GROUP_SIZE):
        better = (i1 != k) & (v[k] > t2)
        i2 = jnp.where(better, k, i2)
        t2 = jnp.where(better, v[k], t2)
    denom = t1 + t2
    w1 = t1 / denom
    w2 = t2 / denom

    rows = []
    for g in range(N_GROUPS):
        for k in range(GROUP_SIZE):
            w = jnp.where(i1 == k, w1, 0.0) + jnp.where(i2 == k, w2, 0.0)
            rows.append(jnp.where(sel == g, w, 0.0))
    return jnp.concatenate(rows, axis=0)


def _mixer_tail(a, x, w_ref, lng_ref, lnb_ref, rwh_ref, rwl_ref, rb_ref, x1_ref, gates_ref):
    y = jnp.dot(a, w_ref[...], preferred_element_type=F32)
    x1 = _layer_norm(ALPHA * x + y, lng_ref[...], lnb_ref[...])
    x1_ref[0] = x1
    hi = x1.astype(BF16)
    lo = (x1 - hi.astype(F32)).astype(BF16)
    logits = (jnp.dot(hi, rwh_ref[...], preferred_element_type=F32)
              + jnp.dot(lo, rwh_ref[...], preferred_element_type=F32)
              + jnp.dot(hi, rwl_ref[...], preferred_element_type=F32)) + rb_ref[...]
    gates_t = _route(logits.T[:N_EXPERTS])
    pad = jnp.zeros((LANES - N_EXPERTS, gates_t.shape[1]), F32)
    gates_ref[0] = jnp.concatenate([gates_t, pad], axis=0).T


def _lru_tail_kernel(hf_ref, hb_ref, gate_ref, x_ref, *rest):
    a = ((hf_ref[0] + hb_ref[0]) * gate_ref[0].astype(F32)).astype(BF16)
    _mixer_tail(a, x_ref[0], *rest)


def _attn_tail_kernel(o_ref, x_ref, *rest):
    _mixer_tail(o_ref[0], x_ref[0], *rest)


def _mixer_tail_call(kern, name, acts, act_specs, x, w, ln_g, ln_b, router):
    b, s, d = x.shape
    tm = min(512, s)
    rwh, rwl, rb = router
    c = w.shape[0]
    tok = lambda bi, i: (bi, i, 0)
    const = lambda bi, i: (0, 0)
    in_specs = act_specs(tm) + [
        pl.BlockSpec((1, tm, d), tok),
        pl.BlockSpec((c, d), const),
        pl.BlockSpec((1, d), const), pl.BlockSpec((1, d), const),
        pl.BlockSpec((d, LANES), const), pl.BlockSpec((d, LANES), const),
        pl.BlockSpec((1, LANES), const)]
    out_specs = (pl.BlockSpec((1, tm, d), tok), pl.BlockSpec((1, tm, LANES), tok))
    out_shape = (jax.ShapeDtypeStruct((b, s, d), F32),
                 jax.ShapeDtypeStruct((b, s, LANES), F32))
    return pl.pallas_call(
        kern, grid=(b, s // tm), in_specs=in_specs, out_specs=out_specs, out_shape=out_shape,
        compiler_params=_params("parallel", "parallel"), name=name,
    )(*acts, x, w, ln_g.reshape(1, d), ln_b.reshape(1, d), rwh, rwl, rb)


def _lru_tail(hf, hb, gate, x, w_out, ln_g, ln_b, router):
    c = gate.shape[-1]
    specs = lambda tm: [pl.BlockSpec((1, tm, c), lambda bi, i: (bi, i, 0))] * 3
    return _mixer_tail_call(_lru_tail_kernel, "lru_tail", (hf, hb, gate), specs,
                            x, w_out, ln_g, ln_b, router)


def _attn_tail(o, x, w_o, ln_g, ln_b, router):
    c = o.shape[-1]
    specs = lambda tm: [pl.BlockSpec((1, tm, c), lambda bi, i: (bi, i, 0))]
    return _mixer_tail_call(_attn_tail_kernel, "attn_tail", (o,), specs,
                            x, w_o, ln_g, ln_b, router)


def _qkv_kernel(x_ref, w_ref, qg_ref, kg_ref, cos_ref, sin_ref, q_ref, k_ref, v_ref):
    u = jnp.dot(x_ref[0].astype(BF16), w_ref[...], preferred_element_type=F32)
    cos = cos_ref[...]
    sin = sin_ref[...]
    lane = lax.broadcasted_iota(jnp.int32, cos.shape, 1)
    first_half = (lane % (2 * ROPE_PAIRS)) < ROPE_PAIRS

    def norm_rope(xh, gain, out_scale):
        ms = jnp.mean(xh * xh, axis=-1, keepdims=True)
        xn = xh * lax.rsqrt(ms + RMS_EPS) * gain
        partner = jnp.where(first_half,
                            pltpu.roll(xn, HEAD_DIM - ROPE_PAIRS, axis=1),
                            pltpu.roll(xn, ROPE_PAIRS, axis=1))
        return ((xn * cos + partner * sin) * out_scale).astype(BF16)

    nq = N_Q_HEADS * HEAD_DIM
    nk = N_KV_HEADS * HEAD_DIM
    for h in range(N_Q_HEADS):
        sl = slice(h * HEAD_DIM, (h + 1) * HEAD_DIM)
        q_ref[0, :, sl] = norm_rope(u[:, sl], qg_ref[...], Q_SCALE)
    for h in range(N_KV_HEADS):
        sl = slice(h * HEAD_DIM, (h + 1) * HEAD_DIM)
        k_ref[0, :, sl] = norm_rope(u[:, nq + h * HEAD_DIM:nq + (h + 1) * HEAD_DIM], kg_ref[...],
                                    1.0)
    v_ref[0] = u[:, nq + nk:].astype(BF16)


def _rope_tables(s):
    pos = jnp.arange(s)
    row = (pos // GRID_W).astype(F32)
    col = (pos % GRID_W).astype(F32)
    freqs = ROPE_THETA ** (-jnp.arange(ROPE_PAIRS, dtype=F32) / ROPE_PAIRS)
    ang_r = row[:, None] * freqs
    ang_c = col[:, None] * freqs
    cos = jnp.concatenate([jnp.cos(ang_r)] * 2 + [jnp.cos(ang_c)] * 2, axis=1)
    sin = jnp.concatenate([-jnp.sin(ang_r), jnp.sin(ang_r), -jnp.sin(ang_c), jnp.sin(ang_c)], axis=1)
    return cos, sin


def _qkv_proj(x, w_qkv, q_gain, k_gain, cos, sin):
    b, s, d = x.shape
    tm = min(512, s)
    nq = N_Q_HEADS * HEAD_DIM
    nk = N_KV_HEADS * HEAD_DIM
    tok = lambda bi, i: (bi, i, 0)
    const = lambda bi, i: (0, 0)
    in_specs = [pl.BlockSpec((1, tm, d), tok),
                pl.BlockSpec((d, nq + 2 * nk), const),
                pl.BlockSpec((1, HEAD_DIM), const), pl.BlockSpec((1, HEAD_DIM), const),
                pl.BlockSpec((tm, HEAD_DIM), lambda bi, i: (i, 0)),
                pl.BlockSpec((tm, HEAD_DIM), lambda bi, i: (i, 0))]
    out_specs = (pl.BlockSpec((1, tm, nq), tok), pl.BlockSpec((1, tm, nk), tok),
                 pl.BlockSpec((1, tm, nk), tok))
    out_shape = (jax.ShapeDtypeStruct((b, s, nq), BF16),
                 jax.ShapeDtypeStruct((b, s, nk), BF16),
                 jax.ShapeDtypeStruct((b, s, nk), BF16))
    return pl.pallas_call(
        _qkv_kernel, grid=(b, s // tm), in_specs=in_specs, out_specs=out_specs,
        out_shape=out_shape, compiler_params=_params("parallel", "parallel"), name="qkv_proj",
    )(x, w_qkv, q_gain.reshape(1, HEAD_DIM), k_gain.reshape(1, HEAD_DIM), cos, sin)


def _flash_kernel(q_ref, k_ref, v_ref, o_ref, *, tkv, heads_per_stream):
    tq = q_ref.shape[1]
    s_len = k_ref.shape[1]
    n_streams = Q_PER_KV // heads_per_stream
    rows = heads_per_stream * tq
    qs = [jnp.concatenate([q_ref[0, :, h * HEAD_DIM:(h + 1) * HEAD_DIM]
                           for h in range(st * heads_per_stream, (st + 1) * heads_per_stream)],
                          axis=0) for st in range(n_streams)]

    def body(c, carry):
        start = pl.multiple_of(c * tkv, tkv)
        kc = k_ref[0, pl.ds(start, tkv), :]
        vc = v_ref[0, pl.ds(start, tkv), :]
        out = []
        for st in range(n_streams):
            m, l, acc = carry[st]
            s = lax.dot_general(qs[st], kc, (((1,), (1,)), ((), ())),
                                preferred_element_type=F32)
            m_new = jnp.maximum(m, jnp.max(s, axis=-1, keepdims=True))
            p = jnp.exp2(s - m_new)
            alpha = jnp.exp2(m - m_new)
            l = alpha * l + jnp.sum(p, axis=-1, keepdims=True)
            acc = alpha * acc + jnp.dot(p.astype(BF16), vc, preferred_element_type=F32)
            out.append((m_new, l, acc))
        return tuple(out)

    init = tuple((jnp.full((rows, 1), -jnp.inf, F32), jnp.zeros((rows, 1), F32),
                  jnp.zeros((rows, HEAD_DIM), F32)) for _ in range(n_streams))
    final = init
    for c in range(s_len // tkv):
        final = body(c, final)
    for st in range(n_streams):
        _, l, acc = final[st]
        o = (acc / l).astype(BF16)
        for hh in range(heads_per_stream):
            h = st * heads_per_stream + hh
            o_ref[0, :, h * HEAD_DIM:(h + 1) * HEAD_DIM] = o[hh * tq:(hh + 1) * tq]


def _flash_attention(q, k, v):
    b, s, nq = q.shape
    tq = min(256, s)
    tkv = min(2048, s)
    gw = Q_PER_KV * HEAD_DIM
    in_specs = [pl.BlockSpec((1, tq, gw), lambda bi, g, i: (bi, i, g)),
                pl.BlockSpec((1, s, HEAD_DIM), lambda bi, g, i: (bi, 0, g)),
                pl.BlockSpec((1, s, HEAD_DIM), lambda bi, g, i: (bi, 0, g))]
    out_specs = pl.BlockSpec((1, tq, gw), lambda bi, g, i: (bi, i, g))
    return pl.pallas_call(
        functools.partial(_flash_kernel, tkv=tkv, heads_per_stream=2),
        grid=(b, N_KV_HEADS, s // tq),
        in_specs=in_specs, out_specs=out_specs,
        out_shape=jax.ShapeDtypeStruct((b, s, nq), BF16),
        compiler_params=_params("parallel", "parallel", "parallel"), name="flash_attn",
    )(q, k, v)


def _moe_kernel(x_ref, gates_ref, wg_ref, wu_ref, wd_ref, lng_ref, lnb_ref, o_ref, xb_s, acc_s):
    e = pl.program_id(1)

    @pl.when(e == 0)
    def _():
        xb_s[...] = x_ref[...].astype(BF16)
        acc_s[...] = jnp.zeros_like(acc_s)

    xb = xb_s[...]
    hg = jnp.dot(xb, wg_ref[0], preferred_element_type=F32)
    hu = jnp.dot(xb, wu_ref[0], preferred_element_type=F32)
    h = (hg * jax.nn.sigmoid(hg)) * hu
    y = jnp.dot(h.astype(BF16), wd_ref[0], preferred_element_type=F32)
    gates = gates_ref[...]
    lane = lax.broadcasted_iota(jnp.int32, gates.shape, 1)
    g = jnp.sum(jnp.where(lane == e, gates, 0.0), axis=-1, keepdims=True)
    acc_s[...] += y * g

    @pl.when(e == pl.num_programs(1) - 1)
    def _():
        o_ref[...] = _layer_norm(ALPHA * x_ref[...] + acc_s[...], lng_ref[...], lnb_ref[...])


def _moe(x1, gates, w_gate, w_up, w_down, ln_g, ln_b):
    b, s, d = x1.shape
    n = b * s
    f = w_gate.shape[-1]
    tm = min(512, n)
    x2 = x1.reshape(n, d)
    g2 = gates.reshape(n, LANES)
    in_specs = [pl.BlockSpec((tm, d), lambda i, e: (i, 0)),
                pl.BlockSpec((tm, LANES), lambda i, e: (i, 0)),
                pl.BlockSpec((1, d, f), lambda i, e: (e, 0, 0)),
                pl.BlockSpec((1, d, f), lambda i, e: (e, 0, 0)),
                pl.BlockSpec((1, f, d), lambda i, e: (e, 0, 0)),
                pl.BlockSpec((1, d), lambda i, e: (0, 0)),
                pl.BlockSpec((1, d), lambda i, e: (0, 0))]
    out = pl.pallas_call(
        _moe_kernel, grid=(n // tm, N_EXPERTS), in_specs=in_specs,
        out_specs=pl.BlockSpec((tm, d), lambda i, e: (i, 0)),
        out_shape=jax.ShapeDtypeStruct((n, d), F32),
        scratch_shapes=[pltpu.VMEM((tm, d), BF16), pltpu.VMEM((tm, d), F32)],
        compiler_params=_params("parallel", "arbitrary"), name="moe",
    )(x2, g2, w_gate, w_up, w_down, ln_g.reshape(1, d), ln_b.reshape(1, d))
    return out.reshape(b, s, d)


def _trunk(x, p):
    b, s, d = x.shape
    cos, sin = _rope_tables(s)
    for i in range(DEPTH):
        j = i // 2
        if i % 2 == 0:
            gate, rec = _lru_inproj(x, p["lru_w_in"][j])
            hf, hb = _lru_scan(rec, p["lru_conv_w"][j], p["lru_conv_b"][j], p["lru_w_a"][j],
                               p["lru_b_a"][j], p["lru_w_i"][j], p["lru_b_i"][j],
                               p["lru_lambda"][j])
            x1, gates = _lru_tail(hf, hb, gate, x, p["lru_w_out"][j], p["ln1_g"][i],
                                  p["ln1_b"][i], p["router"])
        else:
            q, k, v = _qkv_proj(x, p["attn_w_qkv"][j], p["attn_q_gain"][j], p["attn_k_gain"][j],
                                cos, sin)
            o = _flash_attention(q, k, v)
            x1, gates = _attn_tail(o, x, p["attn_w_o"][j], p["ln1_g"][i], p["ln1_b"][i],
                                   p["router"])
        x = _moe(x1, gates, p["moe_w_gate"][i], p["moe_w_up"][i], p["moe_w_down"][i],
                 p["ln2_g"][i], p["ln2_b"][i])
    return x


def kernel(x_prompt, x_sample, lru_w_in, lru_conv_w, lru_conv_b, lru_w_a, lru_b_a, lru_w_i, lru_b_i,
           lru_lambda, lru_w_out, attn_w_qkv, attn_q_gain, attn_k_gain, attn_w_o, ln1_g, ln1_b,
           ln2_g, ln2_b, router_w, router_b, moe_w_gate, moe_w_up, moe_w_down):
    d = router_w.shape[0]
    rw = jnp.zeros((d, LANES), F32).at[:, :N_EXPERTS].set(router_w)
    rw_hi = rw.astype(BF16)
    rw_lo = (rw - rw_hi.astype(F32)).astype(BF16)
    rb = jnp.zeros((1, LANES), F32).at[0, :N_EXPERTS].set(router_b)
    p = dict(
        lru_w_in=lru_w_in.astype(BF16), lru_conv_w=lru_conv_w, lru_conv_b=lru_conv_b,
        lru_w_a=lru_w_a, lru_b_a=lru_b_a, lru_w_i=lru_w_i, lru_b_i=lru_b_i, lru_lambda=lru_lambda,
        lru_w_out=lru_w_out.astype(BF16), attn_w_qkv=attn_w_qkv.astype(BF16),
        attn_q_gain=attn_q_gain, attn_k_gain=attn_k_gain, attn_w_o=attn_w_o.astype(BF16),
        ln1_g=ln1_g, ln1_b=ln1_b, ln2_g=ln2_g, ln2_b=ln2_b, router=(rw_hi, rw_lo, rb),
        moe_w_gate=moe_w_gate.astype(BF16), moe_w_up=moe_w_up.astype(BF16),
        moe_w_down=moe_w_down.astype(BF16))
    return _trunk(x_prompt, p), _trunk(x_sample, p)
```

```python
import functools
import math

import jax
import jax.numpy as jnp
from jax import lax
from jax.experimental import pallas as pl
from jax.experimental.pallas import tpu as pltpu

F32 = jnp.float32
BF16 = jnp.bfloat16

DEPTH = 4
GRID_W = 64
LRU_BW = 128
CONV_W = 4
LRU_C = 8.0
HEAD_DIM = 128
N_Q_HEADS = 8
N_KV_HEADS = 2
Q_PER_KV = N_Q_HEADS // N_KV_HEADS
ROPE_THETA = 10000.0
ROPE_PAIRS = HEAD_DIM // 4
N_EXPERTS = 16
N_GROUPS = 4
GROUP_SIZE = N_EXPERTS // N_GROUPS
Q_SCALE = HEAD_DIM ** -0.5 * math.log2(math.e)
ALPHA = (2 * DEPTH) ** 0.25
LN_EPS = 1e-5
RMS_EPS = 1e-6

LANES = 128
SUBLANES = 8
VMEM_LIMIT_BYTES = 48 * 1024 * 1024
MOE_ROW_STREAMS = 2


def _params(*semantics):
    return pltpu.CompilerParams(dimension_semantics=semantics,
                                vmem_limit_bytes=VMEM_LIMIT_BYTES)


def _gelu_tanh(x):
    c = math.sqrt(2.0 / math.pi)
    return x * (0.5 * (1.0 + jnp.tanh(c * (x + 0.044715 * (x * x * x)))))


def _one_minus_exp(z, exp_z):
    series = -z * (1.0 + z * (1.0 / 2 + z * (1.0 / 6 + z * (1.0 / 24 + z * (1.0 / 120)))))
    return jnp.where(z > -0.01, series, 1.0 - exp_z)


def _layer_norm(z, g, b):
    mu = jnp.mean(z, axis=-1, keepdims=True)
    zc = z - mu
    var = jnp.mean(zc * zc, axis=-1, keepdims=True)
    return zc * lax.rsqrt(var + LN_EPS) * g + b


def _lru_inproj_kernel(x_ref, w_ref, gate_ref, rec_ref):
    c = gate_ref.shape[-1]
    u = jnp.dot(x_ref[0].astype(BF16), w_ref[...], preferred_element_type=F32)
    gate_ref[0] = _gelu_tanh(u[:, :c]).astype(BF16)
    rec_ref[0] = u[:, c:]


def _lru_inproj(x, w_in):
    b, s, d = x.shape
    c = w_in.shape[1] // 2
    tm = min(512, s)
    tok = lambda bi, i: (bi, i, 0)
    return pl.pallas_call(
        _lru_inproj_kernel, grid=(b, s // tm),
        in_specs=[pl.BlockSpec((1, tm, d), tok), pl.BlockSpec((d, 2 * c), lambda bi, i: (0, 0))],
        out_specs=(pl.BlockSpec((1, tm, c), tok), pl.BlockSpec((1, tm, c), tok)),
        out_shape=(jax.ShapeDtypeStruct((b, s, c), BF16), jax.ShapeDtypeStruct((b, s, c), F32)),
        compiler_params=_params("parallel", "parallel"), name="lru_inproj")(x, w_in)


def _lru_scan_kernel(cur_f, prev_f, next_f, cur_b, prev_b, next_b, cw_ref, vec_ref, wg_ref,
                     hf_ref, hb_ref, ext_s, a_s, u_s, hc_s, *, n_chunks):
    i = pl.program_id(1)
    bb, t_len, c_dim = cur_f.shape
    halo = prev_f.shape[1]

    @pl.when(i == 0)
    def _():
        hc_s[...] = jnp.zeros_like(hc_s)

    def gates(cur, prev, nxt, bi, is_first, is_last, d):
        ext_s[pl.ds(0, halo), :] = jnp.where(is_first, 0.0, prev[bi])
        ext_s[pl.ds(halo, t_len), :] = cur[bi]
        ext_s[pl.ds(halo + t_len, halo), :] = jnp.where(is_last, 0.0, nxt[bi])
        c = cw_ref[0:1, :] * ext_s[pl.ds(halo - 1, t_len), :]
        for k in range(1, CONV_W):
            c = c + cw_ref[k:k + 1, :] * ext_s[pl.ds(halo - 1 + k, t_len), :]
        c = c + vec_ref[6:7, :]
        idx = 2 * bi + d
        for kk in range(c_dim // LRU_BW):
            sl = slice(kk * LRU_BW, (kk + 1) * LRU_BW)
            ck = c[:, sl]
            z = jnp.dot(ck.astype(BF16), wg_ref[d, kk], preferred_element_type=F32)
            r = jax.nn.sigmoid(z[:, :LRU_BW] + vec_ref[3 * d:3 * d + 1, sl])
            ig = jax.nn.sigmoid(z[:, LRU_BW:] + vec_ref[3 * d + 1:3 * d + 2, sl])
            lam = vec_ref[3 * d + 2:3 * d + 3, sl]
            softplus = jnp.maximum(-lam, 0.0) + jnp.log1p(jnp.exp(-jnp.abs(lam)))
            log_a = (-LRU_C) * r * softplus
            a = jnp.exp(log_a)
            a_s[idx, :, sl] = a
            u_s[idx, :, sl] = jnp.sqrt(_one_minus_exp(2.0 * log_a, a * a)) * (ig * ck)

    for bi in range(bb):
        gates(cur_f, prev_f, next_f, bi, i == 0, i == n_chunks - 1, 0)
        gates(cur_b, prev_b, next_b, bi, i == n_chunks - 1, i == 0, 1)

    def step(t, hs):
        new = []
        for bi in range(bb):
            for d, out in ((0, hf_ref), (1, hb_ref)):
                idx = 2 * bi + d
                tt = t if d == 0 else t_len - 1 - t
                h = a_s[idx, pl.ds(tt, 1), :] * hs[idx] + u_s[idx, pl.ds(tt, 1), :]
                out[bi, pl.ds(tt, 1), :] = h
                new.append(h)
        return tuple(new)

    init = tuple(hc_s[idx, pl.ds(0, 1), :] for idx in range(2 * bb))
    hs = lax.fori_loop(0, t_len, step, init, unroll=8)
    for idx in range(2 * bb):
        hc_s[idx, pl.ds(0, 1), :] = hs[idx]


def _lru_scan(rec, conv_w, conv_b, w_a, b_a, w_i, b_i, lam):
    b, s, c = rec.shape
    t_len = min(128, s)
    bb = 2
    halo = SUBLANES
    n_chunks = s // t_len
    n_halo_blocks = s // halo
    per_chunk = t_len // halo
    cw = jnp.zeros((SUBLANES, c), F32).at[:CONV_W].set(conv_w)
    vec = jnp.zeros((SUBLANES, c), F32)
    vec = vec.at[0].set(b_a[0]).at[1].set(b_i[0]).at[2].set(lam[0])
    vec = vec.at[3].set(b_a[1]).at[4].set(b_i[1]).at[5].set(lam[1]).at[6].set(conv_b)
    wg = jnp.concatenate([w_a, w_i], axis=-1).astype(BF16)

    def chunk(rev, i):
        return (n_chunks - 1 - i) if rev else i

    def specs(rev):
        return [
            pl.BlockSpec((bb, t_len, c), lambda g, i: (g, chunk(rev, i), 0)),
            pl.BlockSpec((bb, halo, c),
                         lambda g, i: (g, jnp.maximum(chunk(rev, i) * per_chunk - 1, 0), 0)),
            pl.BlockSpec((bb, halo, c),
                         lambda g, i: (g, jnp.minimum((chunk(rev, i) + 1) * per_chunk,
                                                      n_halo_blocks - 1), 0))]

    const = lambda g, i: (0, 0)
    in_specs = specs(False) + specs(True) + [
        pl.BlockSpec((SUBLANES, c), const), pl.BlockSpec((SUBLANES, c), const),
        pl.BlockSpec(wg.shape, lambda g, i: (0, 0, 0, 0))]
    out_specs = (pl.BlockSpec((bb, t_len, c), lambda g, i: (g, i, 0)),
                 pl.BlockSpec((bb, t_len, c), lambda g, i: (g, n_chunks - 1 - i, 0)))
    scratch = [pltpu.VMEM((t_len + 2 * halo, c), F32),
               pltpu.VMEM((2 * bb, t_len, c), F32), pltpu.VMEM((2 * bb, t_len, c), F32),
               pltpu.VMEM((2 * bb, SUBLANES, c), F32)]
    return pl.pallas_call(
        functools.partial(_lru_scan_kernel, n_chunks=n_chunks),
        grid=(b // bb, n_chunks), in_specs=in_specs, out_specs=out_specs,
        out_shape=(jax.ShapeDtypeStruct((b, s, c), F32),) * 2, scratch_shapes=scratch,
        compiler_params=_params("parallel", "arbitrary"), name="lru_scan",
    )(rec, rec, rec, rec, rec, rec, cw, vec, wg)


def _route(logits_t):
    m = jnp.max(logits_t, axis=0, keepdims=True)
    e = jnp.exp(logits_t - m)
    probs = e / jnp.sum(e, axis=0, keepdims=True)
    p = [probs[k:k + 1, :] for k in range(N_EXPERTS)]

    scores = []
    for g in range(N_GROUPS):
        a, b, c, d = p[GROUP_SIZE * g:GROUP_SIZE * (g + 1)]
        top2 = jnp.maximum(jnp.maximum(a + b, a + c), jnp.maximum(a + d, b + c))
        scores.append(jnp.maximum(top2, jnp.maximum(b + d, c + d)))
    best = scores[0]
    sel = jnp.zeros(best.shape, jnp.int32)
    for g in range(1, N_GROUPS):
        better = scores[g] > best
        sel = jnp.where(better, g, sel)
        best = jnp.where(better, scores[g], best)

    v = list(p[:GROUP_SIZE])
    for g in range(1, N_GROUPS):
        for k in range(GROUP_SIZE):
            v[k] = jnp.where(sel == g, p[GROUP_SIZE * g + k], v[k])
    t1 = v[0]
    i1 = jnp.zeros(best.shape, jnp.int32)
    for k in range(1, GROUP_SIZE):
        better = v[k] > t1
        i1 = jnp.where(better, k, i1)
        t1 = jnp.where(better, v[k], t1)
    t2 = jnp.full(best.shape, -jnp.inf, F32)
    i2 = jnp.zeros(best.shape, jnp.int32)
    for k in range(GROUP_SIZE):
        better = (i1 != k) & (v[k] > t2)
        i2 = jnp.where(better, k, i2)
        t2 = jnp.where(better, v[k], t2)
    denom = t1 + t2
    w1 = t1 / denom
    w2 = t2 / denom

    rows = []
    for g in range(N_GROUPS):
        for k in range(GROUP_SIZE):
            w = jnp.where(i1 == k, w1, 0.0) + jnp.where(i2 == k, w2, 0.0)
            rows.append(jnp.where(sel == g, w, 0.0))
    return jnp.concatenate(rows, axis=0)


def _mixer_tail(a, x, w_ref, lng_ref, lnb_ref, rwh_ref, rwl_ref, rb_ref, x1_ref, gates_ref):
    y = jnp.dot(a, w_ref[...], preferred_element_type=F32)
    x1 = _layer_norm(ALPHA * x + y, lng_ref[...], lnb_ref[...])
    x1_ref[0] = x1
    hi = x1.astype(BF16)
    lo = (x1 - hi.astype(F32)).astype(BF16)
    logits = (jnp.dot(hi, rwh_ref[...], preferred_element_type=F32)
              + jnp.dot(lo, rwh_ref[...], preferred_element_type=F32)
              + jnp.dot(hi, rwl_ref[...], preferred_element_type=F32)) + rb_ref[...]
    gates_t = _route(logits.T[:N_EXPERTS])
    pad = jnp.zeros((LANES - N_EXPERTS, gates_t.shape[1]), F32)
    gates_ref[0] = jnp.concatenate([gates_t, pad], axis=0).T


def _lru_tail_kernel(hf_ref, hb_ref, gate_ref, x_ref, *rest):
    a = ((hf_ref[0] + hb_ref[0]) * gate_ref[0].astype(F32)).astype(BF16)
    _mixer_tail(a, x_ref[0], *rest)


def _attn_tail_kernel(o_ref, x_ref, *rest):
    _mixer_tail(o_ref[0], x_ref[0], *rest)


def _mixer_tail_call(kern, name, acts, act_specs, x, w, ln_g, ln_b, router):
    b, s, d = x.shape
    tm = min(512, s)
    rwh, rwl, rb = router
    c = w.shape[0]
    tok = lambda bi, i: (bi, i, 0)
    const = lambda bi, i: (0, 0)
    in_specs = act_specs(tm) + [
        pl.BlockSpec((1, tm, d), tok),
        pl.BlockSpec((c, d), const),
        pl.BlockSpec((1, d), const), pl.BlockSpec((1, d), const),
        pl.BlockSpec((d, LANES), const), pl.BlockSpec((d, LANES), const),
        pl.BlockSpec((1, LANES), const)]
    out_specs = (pl.BlockSpec((1, tm, d), tok), pl.BlockSpec((1, tm, LANES), tok))
    out_shape = (jax.ShapeDtypeStruct((b, s, d), F32),
                 jax.ShapeDtypeStruct((b, s, LANES), F32))
    return pl.pallas_call(
        kern, grid=(b, s // tm), in_specs=in_specs, out_specs=out_specs, out_shape=out_shape,
        compiler_params=_params("parallel", "parallel"), name=name,
    )(*acts, x, w, ln_g.reshape(1, d), ln_b.reshape(1, d), rwh, rwl, rb)


def _lru_tail(hf, hb, gate, x, w_out, ln_g, ln_b, router):
    c = gate.shape[-1]
    specs = lambda tm: [pl.BlockSpec((1, tm, c), lambda bi, i: (bi, i, 0))] * 3
    return _mixer_tail_call(_lru_tail_kernel, "lru_tail", (hf, hb, gate), specs,
                            x, w_out, ln_g, ln_b, router)


def _attn_tail(o, x, w_o, ln_g, ln_b, router):
    c = o.shape[-1]
    specs = lambda tm: [pl.BlockSpec((1, tm, c), lambda bi, i: (bi, i, 0))]
    return _mixer_tail_call(_attn_tail_kernel, "attn_tail", (o,), specs,
                            x, w_o, ln_g, ln_b, router)


def _qkv_kernel(x_ref, w_ref, qg_ref, kg_ref, cos_ref, sin_ref, q_ref, k_ref, v_ref):
    u = jnp.dot(x_ref[0].astype(BF16), w_ref[...], preferred_element_type=F32)
    cos = cos_ref[...]
    sin = sin_ref[...]
    lane = lax.broadcasted_iota(jnp.int32, cos.shape, 1)
    first_half = (lane % (2 * ROPE_PAIRS)) < ROPE_PAIRS

    def norm_rope(xh, gain, out_scale):
        ms = jnp.mean(xh * xh, axis=-1, keepdims=True)
        xn = xh * lax.rsqrt(ms + RMS_EPS) * gain
        partner = jnp.where(first_half,
                            pltpu.roll(xn, HEAD_DIM - ROPE_PAIRS, axis=1),
                            pltpu.roll(xn, ROPE_PAIRS, axis=1))
        return ((xn * cos + partner * sin) * out_scale).astype(BF16)

    nq = N_Q_HEADS * HEAD_DIM
    nk = N_KV_HEADS * HEAD_DIM
    for h in range(N_Q_HEADS):
        sl = slice(h * HEAD_DIM, (h + 1) * HEAD_DIM)
        q_ref[0, :, sl] = norm_rope(u[:, sl], qg_ref[...], Q_SCALE)
    for h in range(N_KV_HEADS):
        sl = slice(h * HEAD_DIM, (h + 1) * HEAD_DIM)
        k_ref[0, :, sl] = norm_rope(u[:, nq + h * HEAD_DIM:nq + (h + 1) * HEAD_DIM], kg_ref[...],
                                    1.0)
    v_ref[0] = u[:, nq + nk:].astype(BF16)


def _rope_tables(s):
    pos = jnp.arange(s)
    row = (pos // GRID_W).astype(F32)
    col = (pos % GRID_W).astype(F32)
    freqs = ROPE_THETA ** (-jnp.arange(ROPE_PAIRS, dtype=F32) / ROPE_PAIRS)
    ang_r = row[:, None] * freqs
    ang_c = col[:, None] * freqs
    cos = jnp.concatenate([jnp.cos(ang_r)] * 2 + [jnp.cos(ang_c)] * 2, axis=1)
    sin = jnp.concatenate([-jnp.sin(ang_r), jnp.sin(ang_r), -jnp.sin(ang_c), jnp.sin(ang_c)], axis=1)
    return cos, sin


def _qkv_proj(x, w_qkv, q_gain, k_gain, cos, sin):
    b, s, d = x.shape
    tm = min(512, s)
    nq = N_Q_HEADS * HEAD_DIM
    nk = N_KV_HEADS * HEAD_DIM
    tok = lambda bi, i: (bi, i, 0)
    const = lambda bi, i: (0, 0)
    in_specs = [pl.BlockSpec((1, tm, d), tok),
                pl.BlockSpec((d, nq + 2 * nk), const),
                pl.BlockSpec((1, HEAD_DIM), const), pl.BlockSpec((1, HEAD_DIM), const),
                pl.BlockSpec((tm, HEAD_DIM), lambda bi, i: (i, 0)),
                pl.BlockSpec((tm, HEAD_DIM), lambda bi, i: (i, 0))]
    out_specs = (pl.BlockSpec((1, tm, nq), tok), pl.BlockSpec((1, tm, nk), tok),
                 pl.BlockSpec((1, tm, nk), tok))
    out_shape = (jax.ShapeDtypeStruct((b, s, nq), BF16),
                 jax.ShapeDtypeStruct((b, s, nk), BF16),
                 jax.ShapeDtypeStruct((b, s, nk), BF16))
    return pl.pallas_call(
        _qkv_kernel, grid=(b, s // tm), in_specs=in_specs, out_specs=out_specs,
        out_shape=out_shape, compiler_params=_params("parallel", "parallel"), name="qkv_proj",
    )(x, w_qkv, q_gain.reshape(1, HEAD_DIM), k_gain.reshape(1, HEAD_DIM), cos, sin)


def _flash_kernel(q_ref, k_ref, v_ref, o_ref, *, tkv, heads_per_stream):
    tq = q_ref.shape[1]
    s_len = k_ref.shape[1]
    n_streams = Q_PER_KV // heads_per_stream
    rows = heads_per_stream * tq
    qs = [jnp.concatenate([q_ref[0, :, h * HEAD_DIM:(h + 1) * HEAD_DIM]
                           for h in range(st * heads_per_stream, (st + 1) * heads_per_stream)],
                          axis=0) for st in range(n_streams)]

    def body(c, carry):
        start = pl.multiple_of(c * tkv, tkv)
        kc = k_ref[0, pl.ds(start, tkv), :]
        vc = v_ref[0, pl.ds(start, tkv), :]
        out = []
        for st in range(n_streams):
            m, l, acc = carry[st]
            s = lax.dot_general(qs[st], kc, (((1,), (1,)), ((), ())),
                                preferred_element_type=F32)
            m_new = jnp.maximum(m, jnp.max(s, axis=-1, keepdims=True))
            p = jnp.exp2(s - m_new)
            alpha = jnp.exp2(m - m_new)
            l = alpha * l + jnp.sum(p, axis=-1, keepdims=True)
            acc = alpha * acc + jnp.dot(p.astype(BF16), vc, preferred_element_type=F32)
            out.append((m_new, l, acc))
        return tuple(out)

    init = tuple((jnp.full((rows, 1), -jnp.inf, F32), jnp.zeros((rows, 1), F32),
                  jnp.zeros((rows, HEAD_DIM), F32)) for _ in range(n_streams))
    final = init
    for c in range(s_len // tkv):
        final = body(c, final)
    for st in range(n_streams):
        _, l, acc = final[st]
        o = (acc / l).astype(BF16)
        for hh in range(heads_per_stream):
            h = st * heads_per_stream + hh
            o_ref[0, :, h * HEAD_DIM:(h + 1) * HEAD_DIM] = o[hh * tq:(hh + 1) * tq]


def _flash_attention(q, k, v):
    b, s, nq = q.shape
    tq = min(256, s)
    tkv = min(2048, s)
    gw = Q_PER_KV * HEAD_DIM
    in_specs = [pl.BlockSpec((1, tq, gw), lambda bi, g, i: (bi, i, g)),
                pl.BlockSpec((1, s, HEAD_DIM), lambda bi, g, i: (bi, 0, g)),
                pl.BlockSpec((1, s, HEAD_DIM), lambda bi, g, i: (bi, 0, g))]
    out_specs = pl.BlockSpec((1, tq, gw), lambda bi, g, i: (bi, i, g))
    return pl.pallas_call(
        functools.partial(_flash_kernel, tkv=tkv, heads_per_stream=2),
        grid=(b, N_KV_HEADS, s // tq),
        in_specs=in_specs, out_specs=out_specs,
        out_shape=jax.ShapeDtypeStruct((b, s, nq), BF16),
        compiler_params=_params("parallel", "parallel", "parallel"), name="flash_attn",
    )(q, k, v)


def _moe_kernel(x_ref, gates_ref, wg_ref, wu_ref, wd_ref, lng_ref, lnb_ref, o_ref, xb_s, acc_s):
    e = pl.program_id(1)

    @pl.when(e == 0)
    def _():
        xb_s[...] = x_ref[...].astype(BF16)
        acc_s[...] = jnp.zeros_like(acc_s)

    rows = xb_s.shape[0] // MOE_ROW_STREAMS
    for st in range(MOE_ROW_STREAMS):
        rs = pl.ds(st * rows, rows)
        xb = xb_s[rs, :]
        hg = jnp.dot(xb, wg_ref[0], preferred_element_type=F32)
        hu = jnp.dot(xb, wu_ref[0], preferred_element_type=F32)
        h = (hg * jax.nn.sigmoid(hg)) * hu
        y = jnp.dot(h.astype(BF16), wd_ref[0], preferred_element_type=F32)
        gates = gates_ref[rs, :]
        lane = lax.broadcasted_iota(jnp.int32, gates.shape, 1)
        g = jnp.sum(jnp.where(lane == e, gates, 0.0), axis=-1, keepdims=True)
        acc_s[rs, :] += y * g

    @pl.when(e == pl.num_programs(1) - 1)
    def _():
        o_ref[...] = _layer_norm(ALPHA * x_ref[...] + acc_s[...], lng_ref[...], lnb_ref[...])


def _moe(x1, gates, w_gate, w_up, w_down, ln_g, ln_b):
    b, s, d = x1.shape
    n = b * s
    f = w_gate.shape[-1]
    tm = min(1024, n)
    x2 = x1.reshape(n, d)
    g2 = gates.reshape(n, LANES)
    in_specs = [pl.BlockSpec((tm, d), lambda i, e: (i, 0)),
                pl.BlockSpec((tm, LANES), lambda i, e: (i, 0)),
                pl.BlockSpec((1, d, f), lambda i, e: (e, 0, 0)),
                pl.BlockSpec((1, d, f), lambda i, e: (e, 0, 0)),
                pl.BlockSpec((1, f, d), lambda i, e: (e, 0, 0)),
                pl.BlockSpec((1, d), lambda i, e: (0, 0)),
                pl.BlockSpec((1, d), lambda i, e: (0, 0))]
    out = pl.pallas_call(
        _moe_kernel, grid=(n // tm, N_EXPERTS), in_specs=in_specs,
        out_specs=pl.BlockSpec((tm, d), lambda i, e: (i, 0)),
        out_shape=jax.ShapeDtypeStruct((n, d), F32),
        scratch_shapes=[pltpu.VMEM((tm, d), BF16), pltpu.VMEM((tm, d), F32)],
        compiler_params=_params("parallel", "arbitrary"), name="moe",
    )(x2, g2, w_gate, w_up, w_down, ln_g.reshape(1, d), ln_b.reshape(1, d))
    return out.reshape(b, s, d)


def _trunk(x, p):
    b, s, d = x.shape
    cos, sin = _rope_tables(s)
    for i in range(DEPTH):
        j = i // 2
        if i % 2 == 0:
            gate, rec = _lru_inproj(x, p["lru_w_in"][j])
            hf, hb = _lru_scan(rec, p["lru_conv_w"][j], p["lru_conv_b"][j], p["lru_w_a"][j],
                               p["lru_b_a"][j], p["lru_w_i"][j], p["lru_b_i"][j],
                               p["lru_lambda"][j])
            x1, gates = _lru_tail(hf, hb, gate, x, p["lru_w_out"][j], p["ln1_g"][i],
                                  p["ln1_b"][i], p["router"])
        else:
            q, k, v = _qkv_proj(x, p["attn_w_qkv"][j], p["attn_q_gain"][j], p["attn_k_gain"][j],
                                cos, sin)
            o = _flash_attention(q, k, v)
            x1, gates = _attn_tail(o, x, p["attn_w_o"][j], p["ln1_g"][i], p["ln1_b"][i],
                                   p["router"])
        x = _moe(x1, gates, p["moe_w_gate"][i], p["moe_w_up"][i], p["moe_w_down"][i],
                 p["ln2_g"][i], p["ln2_b"][i])
    return x


def kernel(x_prompt, x_sample, lru_w_in, lru_conv_w, lru_conv_b, lru_w_a, lru_b_a, lru_w_i, lru_b_i,
           lru_lambda, lru_w_out, attn_w_qkv, attn_q_gain, attn_k_gain, attn_w_o, ln1_g, ln1_b,
           ln2_g, ln2_b, router_w, router_b, moe_w_gate, moe_w_up, moe_w_down):
    d = router_w.shape[0]
    rw = jnp.zeros((d, LANES), F32).at[:, :N_EXPERTS].set(router_w)
    rw_hi = rw.astype(BF16)
    rw_lo = (rw - rw_hi.astype(F32)).astype(BF16)
    rb = jnp.zeros((1, LANES), F32).at[0, :N_EXPERTS].set(router_b)
    p = dict(
        lru_w_in=lru_w_in.astype(BF16), lru_conv_w=lru_conv_w, lru_conv_b=lru_conv_b,
        lru_w_a=lru_w_a, lru_b_a=lru_b_a, lru_w_i=lru_w_i, lru_b_i=lru_b_i, lru_lambda=lru_lambda,
        lru_w_out=lru_w_out.astype(BF16), attn_w_qkv=attn_w_qkv.astype(BF16),
        attn_q_gain=attn_q_gain, attn_k_gain=attn_k_gain, attn_w_o=attn_w_o.astype(BF16),
        ln1_g=ln1_g, ln1_b=ln1_b, ln2_g=ln2_g, ln2_b=ln2_b, router=(rw_hi, rw_lo, rb),
        moe_w_gate=moe_w_gate.astype(BF16), moe_w_up=moe_w_up.astype(BF16),
        moe_w_down=moe_w_down.astype(BF16))
    return _trunk(x_prompt, p), _trunk(x_sample, p)
```

```python
import functools
import math

import jax
import jax.numpy as jnp
from jax import lax
from jax.experimental import pallas as pl
from jax.experimental.pallas import tpu as pltpu

F32 = jnp.float32
BF16 = jnp.bfloat16

DEPTH = 4
GRID_W = 64
LRU_BW = 128
CONV_W = 4
LRU_C = 8.0
HEAD_DIM = 128
N_Q_HEADS = 8
N_KV_HEADS = 2
Q_PER_KV = N_Q_HEADS // N_KV_HEADS
ROPE_THETA = 10000.0
ROPE_PAIRS = HEAD_DIM // 4
N_EXPERTS = 16
N_GROUPS = 4
GROUP_SIZE = N_EXPERTS // N_GROUPS
Q_SCALE = HEAD_DIM ** -0.5 * math.log2(math.e)
ALPHA = (2 * DEPTH) ** 0.25
LN_EPS = 1e-5
RMS_EPS = 1e-6

LANES = 128
SUBLANES = 8
VMEM_LIMIT_BYTES = 48 * 1024 * 1024
MOE_ROW_STREAMS = 2


def _params(*semantics):
    return pltpu.CompilerParams(dimension_semantics=semantics,
                                vmem_limit_bytes=VMEM_LIMIT_BYTES)


def _gelu_tanh(x):
    c = math.sqrt(2.0 / math.pi)
    return x * (0.5 * (1.0 + jnp.tanh(c * (x + 0.044715 * (x * x * x)))))


def _one_minus_exp(z, exp_z):
    series = -z * (1.0 + z * (1.0 / 2 + z * (1.0 / 6 + z * (1.0 / 24 + z * (1.0 / 120)))))
    return jnp.where(z > -0.01, series, 1.0 - exp_z)


def _layer_norm(z, g, b):
    mu = jnp.mean(z, axis=-1, keepdims=True)
    zc = z - mu
    var = jnp.mean(zc * zc, axis=-1, keepdims=True)
    return zc * lax.rsqrt(var + LN_EPS) * g + b


def _lru_inproj_kernel(x_ref, w_ref, gate_ref, rec_ref):
    c = gate_ref.shape[-1]
    u = jnp.dot(x_ref[0].astype(BF16), w_ref[...], preferred_element_type=F32)
    gate_ref[0] = _gelu_tanh(u[:, :c]).astype(BF16)
    rec_ref[0] = u[:, c:]


def _lru_inproj(x, w_in):
    b, s, d = x.shape
    c = w_in.shape[1] // 2
    tm = min(512, s)
    tok = lambda bi, i: (bi, i, 0)
    return pl.pallas_call(
        _lru_inproj_kernel, grid=(b, s // tm),
        in_specs=[pl.BlockSpec((1, tm, d), tok), pl.BlockSpec((d, 2 * c), lambda bi, i: (0, 0))],
        out_specs=(pl.BlockSpec((1, tm, c), tok), pl.BlockSpec((1, tm, c), tok)),
        out_shape=(jax.ShapeDtypeStruct((b, s, c), BF16), jax.ShapeDtypeStruct((b, s, c), F32)),
        compiler_params=_params("parallel", "parallel"), name="lru_inproj")(x, w_in)


def _lru_scan_kernel(cur_f, prev_f, next_f, cur_b, prev_b, next_b, cw_ref, vec_ref, wg_ref,
                     hf_ref, hb_ref, ext_s, a_s, u_s, hc_s, *, n_chunks):
    i = pl.program_id(1)
    bb, t_len, c_dim = cur_f.shape
    halo = prev_f.shape[1]

    @pl.when(i == 0)
    def _():
        hc_s[...] = jnp.zeros_like(hc_s)

    def gates(cur, prev, nxt, bi, is_first, is_last, d):
        ext_s[pl.ds(0, halo), :] = jnp.where(is_first, 0.0, prev[bi])
        ext_s[pl.ds(halo, t_len), :] = cur[bi]
        ext_s[pl.ds(halo + t_len, halo), :] = jnp.where(is_last, 0.0, nxt[bi])
        c = cw_ref[0:1, :] * ext_s[pl.ds(halo - 1, t_len), :]
        for k in range(1, CONV_W):
            c = c + cw_ref[k:k + 1, :] * ext_s[pl.ds(halo - 1 + k, t_len), :]
        c = c + vec_ref[6:7, :]
        idx = 2 * bi + d
        for kk in range(c_dim // LRU_BW):
            sl = slice(kk * LRU_BW, (kk + 1) * LRU_BW)
            ck = c[:, sl]
            z = jnp.dot(ck.astype(BF16), wg_ref[d, kk], preferred_element_type=F32)
            r = jax.nn.sigmoid(z[:, :LRU_BW] + vec_ref[3 * d:3 * d + 1, sl])
            ig = jax.nn.sigmoid(z[:, LRU_BW:] + vec_ref[3 * d + 1:3 * d + 2, sl])
            lam = vec_ref[3 * d + 2:3 * d + 3, sl]
            softplus = jnp.maximum(-lam, 0.0) + jnp.log1p(jnp.exp(-jnp.abs(lam)))
            log_a = (-LRU_C) * r * softplus
            a = jnp.exp(log_a)
            a_s[idx, :, sl] = a
            u_s[idx, :, sl] = jnp.sqrt(_one_minus_exp(2.0 * log_a, a * a)) * (ig * ck)

    for bi in range(bb):
        gates(cur_f, prev_f, next_f, bi, i == 0, i == n_chunks - 1, 0)
        gates(cur_b, prev_b, next_b, bi, i == n_chunks - 1, i == 0, 1)

    def step(t, hs):
        new = []
        for bi in range(bb):
            for d, out in ((0, hf_ref), (1, hb_ref)):
                idx = 2 * bi + d
                tt = t if d == 0 else t_len - 1 - t
                h = a_s[idx, pl.ds(tt, 1), :] * hs[idx] + u_s[idx, pl.ds(tt, 1), :]
                out[bi, pl.ds(tt, 1), :] = h
                new.append(h)
        return tuple(new)

    init = tuple(hc_s[idx, pl.ds(0, 1), :] for idx in range(2 * bb))
    hs = lax.fori_loop(0, t_len, step, init, unroll=8)
    for idx in range(2 * bb):
        hc_s[idx, pl.ds(0, 1), :] = hs[idx]


def _lru_scan(rec, conv_w, conv_b, w_a, b_a, w_i, b_i, lam):
    b, s, c = rec.shape
    t_len = min(128, s)
    bb = 2
    halo = SUBLANES
    n_chunks = s // t_len
    n_halo_blocks = s // halo
    per_chunk = t_len // halo
    cw = jnp.zeros((SUBLANES, c), F32).at[:CONV_W].set(conv_w)
    vec = jnp.zeros((SUBLANES, c), F32)
    vec = vec.at[0].set(b_a[0]).at[1].set(b_i[0]).at[2].set(lam[0])
    vec = vec.at[3].set(b_a[1]).at[4].set(b_i[1]).at[5].set(lam[1]).at[6].set(conv_b)
    wg = jnp.concatenate([w_a, w_i], axis=-1).astype(BF16)

    def chunk(rev, i):
        return (n_chunks - 1 - i) if rev else i

    def specs(rev):
        return [
            pl.BlockSpec((bb, t_len, c), lambda g, i: (g, chunk(rev, i), 0)),
            pl.BlockSpec((bb, halo, c),
                         lambda g, i: (g, jnp.maximum(chunk(rev, i) * per_chunk - 1, 0), 0)),
            pl.BlockSpec((bb, halo, c),
                         lambda g, i: (g, jnp.minimum((chunk(rev, i) + 1) * per_chunk,
                                                      n_halo_blocks - 1), 0))]

    const = lambda g, i: (0, 0)
    in_specs = specs(False) + specs(True) + [
        pl.BlockSpec((SUBLANES, c), const), pl.BlockSpec((SUBLANES, c), const),
        pl.BlockSpec(wg.shape, lambda g, i: (0, 0, 0, 0))]
    out_specs = (pl.BlockSpec((bb, t_len, c), lambda g, i: (g, i, 0)),
                 pl.BlockSpec((bb, t_len, c), lambda g, i: (g, n_chunks - 1 - i, 0)))
    scratch = [pltpu.VMEM((t_len + 2 * halo, c), F32),
               pltpu.VMEM((2 * bb, t_len, c), F32), pltpu.VMEM((2 * bb, t_len, c), F32),
               pltpu.VMEM((2 * bb, SUBLANES, c), F32)]
    return pl.pallas_call(
        functools.partial(_lru_scan_kernel, n_chunks=n_chunks),
        grid=(b // bb, n_chunks), in_specs=in_specs, out_specs=out_specs,
        out_shape=(jax.ShapeDtypeStruct((b, s, c), F32),) * 2, scratch_shapes=scratch,
        compiler_params=_params("parallel", "arbitrary"), name="lru_scan",
    )(rec, rec, rec, rec, rec, rec, cw, vec, wg)


def _route(logits_t):
    m = jnp.max(logits_t, axis=0, keepdims=True)
    e = jnp.exp(logits_t - m)
    probs = e / jnp.sum(e, axis=0, keepdims=True)
    p = [probs[k:k + 1, :] for k in range(N_EXPERTS)]

    scores = []
    for g in range(N_GROUPS):
        a, b, c, d = p[GROUP_SIZE * g:GROUP_SIZE * (g + 1)]
        top2 = jnp.maximum(jnp.maximum(a + b, a + c), jnp.maximum(a + d, b + c))
        scores.append(jnp.maximum(top2, jnp.maximum(b + d, c + d)))
    best = scores[0]
    sel = jnp.zeros(best.shape, jnp.int32)
    for g in range(1, N_GROUPS):
        better = scores[g] > best
        sel = jnp.where(better, g, sel)
        best = jnp.where(better, scores[g], best)

    v = list(p[:GROUP_SIZE])
    for g in range(1, N_GROUPS):
        for k in range(GROUP_SIZE):
            v[k] = jnp.where(sel == g, p[GROUP_SIZE * g + k], v[k])
    t1 = v[0]
    i1 = jnp.zeros(best.shape, jnp.int32)
    for k in range(1, GROUP_SIZE):
        better = v[k] > t1
        i1 = jnp.where(better, k, i1)
        t1 = jnp.where(better, v[k], t1)
    t2 = jnp.full(best.shape, -jnp.inf, F32)
    i2 = jnp.zeros(best.shape, jnp.int32)
    for k in range(GROUP_SIZE):
        better = (i1 != k) & (v[k] > t2)
        i2 = jnp.where(better, k, i2)
        t2 = jnp.where(better, v[k], t2)
    denom = t1 + t2
    w1 = t1 / denom
    w2 = t2 / denom

    rows = []
    for g in range(N_GROUPS):
        for k in range(GROUP_SIZE):
            w = jnp.where(i1 == k, w1, 0.0) + jnp.where(i2 == k, w2, 0.0)
            rows.append(jnp.where(sel == g, w, 0.0))
    return jnp.concatenate(rows, axis=0)


def _mixer_tail(a, x, w_ref, lng_ref, lnb_ref, rwh_ref, rwl_ref, rb_ref, x1_ref, gates_ref):
    y = jnp.dot(a, w_ref[...], preferred_element_type=F32)
    x1 = _layer_norm(ALPHA * x + y, lng_ref[...], lnb_ref[...])
    x1_ref[0] = x1
    hi = x1.astype(BF16)
    lo = (x1 - hi.astype(F32)).astype(BF16)
    logits = (jnp.dot(hi, rwh_ref[...], preferred_element_type=F32)
              + jnp.dot(lo, rwh_ref[...], preferred_element_type=F32)
              + jnp.dot(hi, rwl_ref[...], preferred_element_type=F32)) + rb_ref[...]
    gates_t = _route(logits.T[:N_EXPERTS])
    pad = jnp.zeros((LANES - N_EXPERTS, gates_t.shape[1]), F32)
    gates_ref[0] = jnp.concatenate([gates_t, pad], axis=0).T


def _lru_tail_kernel(hf_ref, hb_ref, gate_ref, x_ref, *rest):
    a = ((hf_ref[0] + hb_ref[0]) * gate_ref[0].astype(F32)).astype(BF16)
    _mixer_tail(a, x_ref[0], *rest)


def _attn_tail_kernel(o_ref, x_ref, *rest):
    _mixer_tail(o_ref[0], x_ref[0], *rest)


def _mixer_tail_call(kern, name, acts, act_specs, x, w, ln_g, ln_b, router):
    b, s, d = x.shape
    tm = min(512, s)
    rwh, rwl, rb = router
    c = w.shape[0]
    tok = lambda bi, i: (bi, i, 0)
    const = lambda bi, i: (0, 0)
    in_specs = act_specs(tm) + [
        pl.BlockSpec((1, tm, d), tok),
        pl.BlockSpec((c, d), const),
        pl.BlockSpec((1, d), const), pl.BlockSpec((1, d), const),
        pl.BlockSpec((d, LANES), const), pl.BlockSpec((d, LANES), const),
        pl.BlockSpec((1, LANES), const)]
    out_specs = (pl.BlockSpec((1, tm, d), tok), pl.BlockSpec((1, tm, LANES), tok))
    out_shape = (jax.ShapeDtypeStruct((b, s, d), F32),
                 jax.ShapeDtypeStruct((b, s, LANES), F32))
    return pl.pallas_call(
        kern, grid=(b, s // tm), in_specs=in_specs, out_specs=out_specs, out_shape=out_shape,
        compiler_params=_params("parallel", "parallel"), name=name,
    )(*acts, x, w, ln_g.reshape(1, d), ln_b.reshape(1, d), rwh, rwl, rb)


def _lru_tail(hf, hb, gate, x, w_out, ln_g, ln_b, router):
    c = gate.shape[-1]
    specs = lambda tm: [pl.BlockSpec((1, tm, c), lambda bi, i: (bi, i, 0))] * 3
    return _mixer_tail_call(_lru_tail_kernel, "lru_tail", (hf, hb, gate), specs,
                            x, w_out, ln_g, ln_b, router)


def _attn_tail(o, x, w_o, ln_g, ln_b, router):
    c = o.shape[-1]
    specs = lambda tm: [pl.BlockSpec((1, tm, c), lambda bi, i: (bi, i, 0))]
    return _mixer_tail_call(_attn_tail_kernel, "attn_tail", (o,), specs,
                            x, w_o, ln_g, ln_b, router)


def _qkv_kernel(x_ref, w_ref, qg_ref, kg_ref, cos_ref, sin_ref, q_ref, k_ref, v_ref):
    u = jnp.dot(x_ref[0].astype(BF16), w_ref[...], preferred_element_type=F32)
    cos = cos_ref[...]
    sin = sin_ref[...]
    lane = lax.broadcasted_iota(jnp.int32, cos.shape, 1)
    first_half = (lane % (2 * ROPE_PAIRS)) < ROPE_PAIRS

    def norm_rope(xh, gain, out_scale):
        ms = jnp.mean(xh * xh, axis=-1, keepdims=True)
        xn = xh * lax.rsqrt(ms + RMS_EPS) * gain
        partner = jnp.where(first_half,
                            pltpu.roll(xn, HEAD_DIM - ROPE_PAIRS, axis=1),
                            pltpu.roll(xn, ROPE_PAIRS, axis=1))
        return ((xn * cos + partner * sin) * out_scale).astype(BF16)

    nq = N_Q_HEADS * HEAD_DIM
    nk = N_KV_HEADS * HEAD_DIM
    for h in range(N_Q_HEADS):
        sl = slice(h * HEAD_DIM, (h + 1) * HEAD_DIM)
        q_ref[0, :, sl] = norm_rope(u[:, sl], qg_ref[...], Q_SCALE)
    for h in range(N_KV_HEADS):
        sl = slice(h * HEAD_DIM, (h + 1) * HEAD_DIM)
        k_ref[0, :, sl] = norm_rope(u[:, nq + h * HEAD_DIM:nq + (h + 1) * HEAD_DIM], kg_ref[...],
                                    1.0)
    v_ref[0] = u[:, nq + nk:].astype(BF16)


def _rope_tables(s):
    pos = jnp.arange(s)
    row = (pos // GRID_W).astype(F32)
    col = (pos % GRID_W).astype(F32)
    freqs = ROPE_THETA ** (-jnp.arange(ROPE_PAIRS, dtype=F32) / ROPE_PAIRS)
    ang_r = row[:, None] * freqs
    ang_c = col[:, None] * freqs
    cos = jnp.concatenate([jnp.cos(ang_r)] * 2 + [jnp.cos(ang_c)] * 2, axis=1)
    sin = jnp.concatenate([-jnp.sin(ang_r), jnp.sin(ang_r), -jnp.sin(ang_c), jnp.sin(ang_c)], axis=1)
    return cos, sin


def _qkv_proj(x, w_qkv, q_gain, k_gain, cos, sin):
    b, s, d = x.shape
    tm = min(512, s)
    nq = N_Q_HEADS * HEAD_DIM
    nk = N_KV_HEADS * HEAD_DIM
    tok = lambda bi, i: (bi, i, 0)
    const = lambda bi, i: (0, 0)
    in_specs = [pl.BlockSpec((1, tm, d), tok),
                pl.BlockSpec((d, nq + 2 * nk), const),
                pl.BlockSpec((1, HEAD_DIM), const), pl.BlockSpec((1, HEAD_DIM), const),
                pl.BlockSpec((tm, HEAD_DIM), lambda bi, i: (i, 0)),
                pl.BlockSpec((tm, HEAD_DIM), lambda bi, i: (i, 0))]
    out_specs = (pl.BlockSpec((1, tm, nq), tok), pl.BlockSpec((1, tm, nk), tok),
                 pl.BlockSpec((1, tm, nk), tok))
    out_shape = (jax.ShapeDtypeStruct((b, s, nq), BF16),
                 jax.ShapeDtypeStruct((b, s, nk), BF16),
                 jax.ShapeDtypeStruct((b, s, nk), BF16))
    return pl.pallas_call(
        _qkv_kernel, grid=(b, s // tm), in_specs=in_specs, out_specs=out_specs,
        out_shape=out_shape, compiler_params=_params("parallel", "parallel"), name="qkv_proj",
    )(x, w_qkv, q_gain.reshape(1, HEAD_DIM), k_gain.reshape(1, HEAD_DIM), cos, sin)


def _flash_kernel(q_ref, k_ref, v_ref, o_ref, *, tkv, heads_per_stream):
    tq = q_ref.shape[1]
    s_len = k_ref.shape[1]
    n_streams = Q_PER_KV // heads_per_stream
    rows = heads_per_stream * tq
    qs = [jnp.concatenate([q_ref[0, :, h * HEAD_DIM:(h + 1) * HEAD_DIM]
                           for h in range(st * heads_per_stream, (st + 1) * heads_per_stream)],
                          axis=0) for st in range(n_streams)]

    def body(c, carry):
        start = pl.multiple_of(c * tkv, tkv)
        kc = k_ref[0, pl.ds(start, tkv), :]
        vc = v_ref[0, pl.ds(start, tkv), :]
        out = []
        for st in range(n_streams):
            m, l, acc = carry[st]
            s = lax.dot_general(qs[st], kc, (((1,), (1,)), ((), ())),
                                preferred_element_type=F32)
            m_new = jnp.maximum(m, jnp.max(s, axis=-1, keepdims=True))
            p = jnp.exp2(s - m_new)
            alpha = jnp.exp2(m - m_new)
            l = alpha * l + jnp.sum(p, axis=-1, keepdims=True)
            acc = alpha * acc + jnp.dot(p.astype(BF16), vc, preferred_element_type=F32)
            out.append((m_new, l, acc))
        return tuple(out)

    init = tuple((jnp.full((rows, 1), -jnp.inf, F32), jnp.zeros((rows, 1), F32),
                  jnp.zeros((rows, HEAD_DIM), F32)) for _ in range(n_streams))
    final = init
    for c in range(s_len // tkv):
        final = body(c, final)
    for st in range(n_streams):
        _, l, acc = final[st]
        o = (acc / l).astype(BF16)
        for hh in range(heads_per_stream):
            h = st * heads_per_stream + hh
            o_ref[0, :, h * HEAD_DIM:(h + 1) * HEAD_DIM] = o[hh * tq:(hh + 1) * tq]


def _flash_attention(q, k, v):
    b, s, nq = q.shape
    tq = min(256, s)
    tkv = min(2048, s)
    gw = Q_PER_KV * HEAD_DIM
    in_specs = [pl.BlockSpec((1, tq, gw), lambda bi, g, i: (bi, i, g)),
                pl.BlockSpec((1, s, HEAD_DIM), lambda bi, g, i: (bi, 0, g)),
                pl.BlockSpec((1, s, HEAD_DIM), lambda bi, g, i: (bi, 0, g))]
    out_specs = pl.BlockSpec((1, tq, gw), lambda bi, g, i: (bi, i, g))
    return pl.pallas_call(
        functools.partial(_flash_kernel, tkv=tkv, heads_per_stream=2),
        grid=(b, N_KV_HEADS, s // tq),
        in_specs=in_specs, out_specs=out_specs,
        out_shape=jax.ShapeDtypeStruct((b, s, nq), BF16),
        compiler_params=_params("parallel", "parallel", "parallel"), name="flash_attn",
    )(q, k, v)


def _moe_kernel(x_ref, gates_ref, wgu_ref, wd_ref, lng_ref, lnb_ref, o_ref, xb_s, acc_s):
    e = pl.program_id(1)
    f = wd_ref.shape[1]

    @pl.when(e == 0)
    def _():
        xb_s[...] = x_ref[...].astype(BF16)
        acc_s[...] = jnp.zeros_like(acc_s)

    rows = xb_s.shape[0] // MOE_ROW_STREAMS
    for st in range(MOE_ROW_STREAMS):
        rs = pl.ds(st * rows, rows)
        xb = xb_s[rs, :]
        hgu = jnp.dot(xb, wgu_ref[0], preferred_element_type=F32)
        hg = hgu[:, :f]
        h = (hg * jax.nn.sigmoid(hg)) * hgu[:, f:]
        y = jnp.dot(h.astype(BF16), wd_ref[0], preferred_element_type=F32)
        gates = gates_ref[rs, :]
        lane = lax.broadcasted_iota(jnp.int32, gates.shape, 1)
        g = jnp.sum(jnp.where(lane == e, gates, 0.0), axis=-1, keepdims=True)
        acc_s[rs, :] += y * g

    @pl.when(e == pl.num_programs(1) - 1)
    def _():
        o_ref[...] = _layer_norm(ALPHA * x_ref[...] + acc_s[...], lng_ref[...], lnb_ref[...])


def _moe(x1, gates, w_gate_up, w_down, ln_g, ln_b):
    b, s, d = x1.shape
    n = b * s
    f = w_down.shape[1]
    tm = min(1024, n)
    x2 = x1.reshape(n, d)
    g2 = gates.reshape(n, LANES)
    in_specs = [pl.BlockSpec((tm, d), lambda i, e: (i, 0)),
                pl.BlockSpec((tm, LANES), lambda i, e: (i, 0)),
                pl.BlockSpec((1, d, 2 * f), lambda i, e: (e, 0, 0)),
                pl.BlockSpec((1, f, d), lambda i, e: (e, 0, 0)),
                pl.BlockSpec((1, d), lambda i, e: (0, 0)),
                pl.BlockSpec((1, d), lambda i, e: (0, 0))]
    out = pl.pallas_call(
        _moe_kernel, grid=(n // tm, N_EXPERTS), in_specs=in_specs,
        out_specs=pl.BlockSpec((tm, d), lambda i, e: (i, 0)),
        out_shape=jax.ShapeDtypeStruct((n, d), F32),
        scratch_shapes=[pltpu.VMEM((tm, d), BF16), pltpu.VMEM((tm, d), F32)],
        compiler_params=_params("parallel", "arbitrary"), name="moe",
    )(x2, g2, w_gate_up, w_down, ln_g.reshape(1, d), ln_b.reshape(1, d))
    return out.reshape(b, s, d)


def _trunk(x, p):
    b, s, d = x.shape
    cos, sin = _rope_tables(s)
    for i in range(DEPTH):
        j = i // 2
        if i % 2 == 0:
            gate, rec = _lru_inproj(x, p["lru_w_in"][j])
            hf, hb = _lru_scan(rec, p["lru_conv_w"][j], p["lru_conv_b"][j], p["lru_w_a"][j],
                               p["lru_b_a"][j], p["lru_w_i"][j], p["lru_b_i"][j],
                               p["lru_lambda"][j])
            x1, gates = _lru_tail(hf, hb, gate, x, p["lru_w_out"][j], p["ln1_g"][i],
                                  p["ln1_b"][i], p["router"])
        else:
            q, k, v = _qkv_proj(x, p["attn_w_qkv"][j], p["attn_q_gain"][j], p["attn_k_gain"][j],
                                cos, sin)
            o = _flash_attention(q, k, v)
            x1, gates = _attn_tail(o, x, p["attn_w_o"][j], p["ln1_g"][i], p["ln1_b"][i],
                                   p["router"])
        x = _moe(x1, gates, p["moe_w_gate_up"][i], p["moe_w_down"][i],
                 p["ln2_g"][i], p["ln2_b"][i])
    return x


def kernel(x_prompt, x_sample, lru_w_in, lru_conv_w, lru_conv_b, lru_w_a, lru_b_a, lru_w_i, lru_b_i,
           lru_lambda, lru_w_out, attn_w_qkv, attn_q_gain, attn_k_gain, attn_w_o, ln1_g, ln1_b,
           ln2_g, ln2_b, router_w, router_b, moe_w_gate, moe_w_up, moe_w_down):
    d = router_w.shape[0]
    rw = jnp.zeros((d, LANES), F32).at[:, :N_EXPERTS].set(router_w)
    rw_hi = rw.astype(BF16)
    rw_lo = (rw - rw_hi.astype(F32)).astype(BF16)
    rb = jnp.zeros((1, LANES), F32).at[0, :N_EXPERTS].set(router_b)
    p = dict(
        lru_w_in=lru_w_in.astype(BF16), lru_conv_w=lru_conv_w, lru_conv_b=lru_conv_b,
        lru_w_a=lru_w_a, lru_b_a=lru_b_a, lru_w_i=lru_w_i, lru_b_i=lru_b_i, lru_lambda=lru_lambda,
        lru_w_out=lru_w_out.astype(BF16), attn_w_qkv=attn_w_qkv.astype(BF16),
        attn_q_gain=attn_q_gain, attn_k_gain=attn_k_gain, attn_w_o=attn_w_o.astype(BF16),
        ln1_g=ln1_g, ln1_b=ln1_b, ln2_g=ln2_g, ln2_b=ln2_b, router=(rw_hi, rw_lo, rb),
        moe_w_gate_up=jnp.concatenate([moe_w_gate, moe_w_up], axis=-1).astype(BF16),
        moe_w_down=moe_w_down.astype(BF16))
    return _trunk(x_prompt, p), _trunk(x_sample, p)
```
